```python
import jax, jax.numpy as jnp
from jax import lax
import numpy as np

D_MODEL = 2048
BATCH = 4
SEQ = 4096
DEPTH = 2

HEAD_DIM = 128
MOBA_HEADS = 8
FOX_HEADS = 8
ROT_DIM = HEAD_DIM // 4
ROPE_THETA = 500000.0
MOBA_BLOCK = 256
MOBA_TOPK = 3
MOBA_Q_CHUNK = 32
Q_BLOCK = 128
MIX_WIDTH = (MOBA_HEADS + FOX_HEADS) * HEAD_DIM
AB_IN_WIDTH = 3 * MIX_WIDTH + FOX_HEADS
MLA_HEADS = 16
MLA_Q_LORA = 512
MLA_KV_LORA = 512
MLA_NOPE_DIM = 128
MLA_ROPE_DIM = 64
MLA_V_DIM = 128
N_EXPERTS = 16
N_GROUPS = 4
EXPERTS_PER_GROUP = N_EXPERTS // N_GROUPS
TOP_K = 2
D_EXPERT = 1408
NORM_EPS = 1e-6

kernel_name = 'hybrid_moba_fox_mla_groupmoe_adaln'


def rms_norm(x, gain):
    xf = x.astype(jnp.float32)
    y = xf * lax.rsqrt(jnp.mean(xf * xf, axis=-1, keepdims=True) + NORM_EPS)
    return (y * gain.astype(jnp.float32)).astype(x.dtype)


def rotary(x, positions, rot_dim):
    half = rot_dim // 2
    inv_freq = ROPE_THETA ** (-jnp.arange(half, dtype=jnp.float32) / half)
    ang = positions.astype(jnp.float32)[:, None] * inv_freq[None, :]
    cos, sin = jnp.cos(ang), jnp.sin(ang)
    xr = x[..., :rot_dim].astype(jnp.float32)
    x1, x2 = xr[..., :half], xr[..., half:]
    rot = jnp.concatenate([x1 * cos - x2 * sin, x2 * cos + x1 * sin], axis=-1)
    return jnp.concatenate([rot.astype(x.dtype), x[..., rot_dim:]], axis=-1)


def causal_block_attention(q, k, v, log_fcum=None):
    B, H, S, Dk = q.shape
    scale = Dk ** -0.5
    kpos = jnp.arange(S)

    def one_block(i):
        t0 = i * Q_BLOCK
        qb = lax.dynamic_slice_in_dim(q, t0, Q_BLOCK, axis=2)
        s = jnp.einsum('bhqd,bhkd->bhqk', qb, k).astype(jnp.float32) * scale
        qpos = t0 + jnp.arange(Q_BLOCK)
        if log_fcum is not None:
            fq = lax.dynamic_slice_in_dim(log_fcum, t0, Q_BLOCK, axis=2)
            s = s + fq[..., :, None] - log_fcum[..., None, :]
        s = jnp.where(kpos[None, :] <= qpos[:, None], s, -jnp.inf)
        p = jax.nn.softmax(s, axis=-1).astype(v.dtype)
        return jnp.einsum('bhqk,bhkd->bhqd', p, v)

    out = lax.map(one_block, jnp.arange(S // Q_BLOCK))
    return out.transpose(1, 2, 0, 3, 4).reshape(B, H, S, v.shape[-1])


def moba_attention(q, k, v):
    B, H, S, D = q.shape
    nb = -(-S // MOBA_BLOCK)
    pad = nb * MOBA_BLOCK - S
    kp = jnp.pad(k, ((0, 0), (0, 0), (0, pad), (0, 0)))
    vp = jnp.pad(v, ((0, 0), (0, 0), (0, pad), (0, 0)))
    kb = kp.reshape(B, H, nb, MOBA_BLOCK, D)
    vb = vp.reshape(B, H, nb, MOBA_BLOCK, D)
    k_mean = jnp.mean(kb.astype(jnp.float32), axis=3)
    n_sel = min(MOBA_TOPK, nb - 1)
    scale = D ** -0.5
    bix = jnp.arange(B)[:, None, None, None]
    hix = jnp.arange(H)[None, :, None, None]

    def one_chunk(ci):
        t0 = ci * MOBA_Q_CHUNK
        blk = t0 // MOBA_BLOCK
        qc = lax.dynamic_slice_in_dim(q, t0, MOBA_Q_CHUNK, axis=2)
        qpos = t0 + jnp.arange(MOBA_Q_CHUNK)
        k_own = lax.dynamic_slice_in_dim(kp, blk * MOBA_BLOCK, MOBA_BLOCK, axis=2)
        v_own = lax.dynamic_slice_in_dim(vp, blk * MOBA_BLOCK, MOBA_BLOCK, axis=2)
        own_pos = blk * MOBA_BLOCK + jnp.arange(MOBA_BLOCK)
        s_own = jnp.einsum('bhcd,bhnd->bhcn', qc, k_own).astype(jnp.float32) * scale
        s_own = jnp.where(own_pos[None, :] <= qpos[:, None], s_own, -jnp.inf)
        if n_sel == 0:
            p_own = jax.nn.softmax(s_own, axis=-1).astype(v.dtype)
            return jnp.einsum('bhcn,bhnd->bhcd', p_own, v_own)
        gate = jnp.einsum('bhcd,bhnd->bhcn', qc.astype(jnp.float32), k_mean)
        gate = jnp.where(jnp.arange(nb) < blk, gate, -jnp.inf)
        _, sel = lax.top_k(gate, n_sel)
        valid = sel < blk
        k_sel = kb[bix, hix, sel]
        v_sel = vb[bix, hix, sel]
        s_sel = jnp.einsum('bhcd,bhcjnd->bhcjn', qc, k_sel).astype(jnp.float32) * scale
        s_sel = jnp.where(valid[..., None], s_sel, -jnp.inf)
        s_sel = s_sel.reshape(B, H, MOBA_Q_CHUNK, n_sel * MOBA_BLOCK)
        p = jax.nn.softmax(jnp.concatenate([s_sel, s_own], axis=-1), axis=-1).astype(v.dtype)
        p_sel = p[..., :n_sel * MOBA_BLOCK].reshape(B, H, MOBA_Q_CHUNK, n_sel, MOBA_BLOCK)
        p_own = p[..., n_sel * MOBA_BLOCK:]
        return (jnp.einsum('bhcjn,bhcjnd->bhcd', p_sel, v_sel)
                + jnp.einsum('bhcn,bhnd->bhcd', p_own, v_own))

    out = lax.map(one_chunk, jnp.arange(S // MOBA_Q_CHUNK))
    return out.transpose(1, 2, 0, 3, 4).reshape(B, H, S, D)


def moba_fox_mixer(h, w_in, forget_bias, w_out, positions):
    B, S, _ = h.shape
    proj = h @ w_in
    s1 = MOBA_HEADS * HEAD_DIM
    s2 = FOX_HEADS * HEAD_DIM
    bounds = [s1, 2 * s1, 3 * s1, 3 * s1 + s2, 3 * s1 + 2 * s2, 3 * s1 + 3 * s2]
    qa, ka, va, qb, kb, vb, f_logit = jnp.split(proj, bounds, axis=-1)

    def heads(t, n):
        return t.reshape(B, S, n, HEAD_DIM).transpose(0, 2, 1, 3)

    qa = rotary(heads(qa, MOBA_HEADS), positions, ROT_DIM)
    ka = rotary(heads(ka, MOBA_HEADS), positions, ROT_DIM)
    va = heads(va, MOBA_HEADS)
    log_f = jax.nn.log_sigmoid((f_logit + forget_bias).astype(jnp.float32))
    fcum = jnp.cumsum(log_f, axis=1).transpose(0, 2, 1)
    oa = moba_attention(qa, ka, va)
    ob = causal_block_attention(heads(qb, FOX_HEADS), heads(kb, FOX_HEADS), heads(vb, FOX_HEADS), fcum)
    o = jnp.concatenate([oa, ob], axis=1)
    return o.transpose(0, 2, 1, 3).reshape(B, S, MIX_WIDTH) @ w_out


def mla_mixer(h, w_dq, q_norm, w_uq, w_dkv, kv_norm, w_ukv, w_out, positions):
    B, S, _ = h.shape
    c_q = rms_norm(h @ w_dq, q_norm)
    q = (c_q @ w_uq).reshape(B, S, MLA_HEADS, MLA_NOPE_DIM + MLA_ROPE_DIM).transpose(0, 2, 1, 3)
    q_nope, q_pe = q[..., :MLA_NOPE_DIM], q[..., MLA_NOPE_DIM:]
    q_pe = rotary(q_pe, positions, MLA_ROPE_DIM)
    kv_in = h @ w_dkv
    c_kv, k_pe = kv_in[..., :MLA_KV_LORA], kv_in[..., MLA_KV_LORA:]
    k_pe = rotary(k_pe[:, None], positions, MLA_ROPE_DIM)
    kv = (rms_norm(c_kv, kv_norm) @ w_ukv).reshape(B, S, MLA_HEADS, MLA_NOPE_DIM + MLA_V_DIM).transpose(0, 2, 1, 3)
    k_nope, v = kv[..., :MLA_NOPE_DIM], kv[..., MLA_NOPE_DIM:]
    qf = jnp.concatenate([q_nope, q_pe], axis=-1)
    kf = jnp.concatenate([k_nope, jnp.broadcast_to(k_pe, (B, MLA_HEADS, S, MLA_ROPE_DIM))], axis=-1)
    o = causal_block_attention(qf, kf, v)
    return o.transpose(0, 2, 1, 3).reshape(B, S, MLA_HEADS * MLA_V_DIM) @ w_out


def moe_ffn(h, router_w, router_bias, w_gate, w_up, w_down):
    B, S, D = h.shape
    t = h.reshape(B * S, D)
    scores = jax.nn.sigmoid((t @ router_w).astype(jnp.float32))
    biased = scores + router_bias.astype(jnp.float32)
    grp_score = lax.top_k(biased.reshape(-1, N_GROUPS, EXPERTS_PER_GROUP), 2)[0].sum(-1)
    g_sel = jnp.argmax(grp_score, axis=-1)
    in_grp = (jnp.arange(N_EXPERTS) // EXPERTS_PER_GROUP)[None, :] == g_sel[:, None]
    _, idx = lax.top_k(jnp.where(in_grp, biased, -jnp.inf), TOP_K)
    w = jnp.take_along_axis(scores, idx, axis=-1)
    w = w / jnp.sum(w, axis=-1, keepdims=True)
    comb = jnp.einsum('nk,nke->ne', w, jax.nn.one_hot(idx, N_EXPERTS, dtype=jnp.float32)).astype(t.dtype)
    y = jnp.zeros_like(t)
    for e in range(N_EXPERTS):
        a = jax.nn.silu(t @ w_gate[e]) * (t @ w_up[e])
        y = y + comb[:, e:e + 1] * (a @ w_down[e])
    return y.reshape(B, S, D)


def setup_inputs(seed: int = 0) -> dict:
    key = jax.random.key(seed)
    ks = jax.random.split(key, 24)

    def nrm(k, shape, scale):
        return jax.random.normal(k, shape, jnp.float32) * scale

    D = D_MODEL
    return {
        'x': nrm(ks[0], (BATCH, SEQ, D), 1.0),
        'c': nrm(ks[1], (BATCH, D), 1.0),
        'ada_w': nrm(ks[2], (DEPTH, D, 6 * D), 0.5 * D ** -0.5),
        'ada_b': nrm(ks[3], (DEPTH, 6 * D), 0.01),
        'norm_mix': 1.0 + nrm(ks[4], (DEPTH, D), 0.05),
        'norm_ffn': 1.0 + nrm(ks[5], (DEPTH, D), 0.05),
        'ab_w_in': nrm(ks[6], (D, AB_IN_WIDTH), D ** -0.5),
        'ab_forget_bias': jax.random.uniform(ks[7], (FOX_HEADS,), jnp.float32, 1.0, 3.0),
        'ab_w_out': nrm(ks[8], (MIX_WIDTH, D), MIX_WIDTH ** -0.5),
        'mla_w_dq': nrm(ks[9], (D, MLA_Q_LORA), D ** -0.5),
        'mla_q_norm': 1.0 + nrm(ks[10], (MLA_Q_LORA,), 0.05),
        'mla_w_uq': nrm(ks[11], (MLA_Q_LORA, MLA_HEADS * (MLA_NOPE_DIM + MLA_ROPE_DIM)), MLA_Q_LORA ** -0.5),
        'mla_w_dkv': nrm(ks[12], (D, MLA_KV_LORA + MLA_ROPE_DIM), D ** -0.5),
        'mla_kv_norm': 1.0 + nrm(ks[13], (MLA_KV_LORA,), 0.05),
        'mla_w_ukv': nrm(ks[14], (MLA_KV_LORA, MLA_HEADS * (MLA_NOPE_DIM + MLA_V_DIM)), MLA_KV_LORA ** -0.5),
        'mla_w_out': nrm(ks[15], (MLA_HEADS * MLA_V_DIM, D), (MLA_HEADS * MLA_V_DIM) ** -0.5),
        'router_w': nrm(ks[16], (D, N_EXPERTS), D ** -0.5),
        'router_bias': nrm(ks[17], (N_EXPERTS,), 0.01),
        'exp_w_gate': nrm(ks[18], (DEPTH, N_EXPERTS, D, D_EXPERT), D ** -0.5),
        'exp_w_up': nrm(ks[19], (DEPTH, N_EXPERTS, D, D_EXPERT), D ** -0.5),
        'exp_w_down': nrm(ks[20], (DEPTH, N_EXPERTS, D_EXPERT, D), D_EXPERT ** -0.5),
        'final_norm': 1.0 + nrm(ks[21], (D,), 0.05),
    }


def reference(x, c, ada_w, ada_b, norm_mix, norm_ffn, ab_w_in, ab_forget_bias, ab_w_out,
              mla_w_dq, mla_q_norm, mla_w_uq, mla_w_dkv, mla_kv_norm, mla_w_ukv, mla_w_out,
              router_w, router_bias, exp_w_gate, exp_w_up, exp_w_down, final_norm):
    S = x.shape[1]
    positions = jnp.arange(S)
    c_act = jax.nn.silu(c)
    for layer in range(DEPTH):
        mod = (c_act @ ada_w[layer] + ada_b[layer])[:, None, :]
        sh_m, sc_m, g_m, sh_f, sc_f, g_f = jnp.split(mod, 6, axis=-1)
        h = rms_norm(x, norm_mix[layer]) * (1 + sc_m) + sh_m
        if layer % 2 == 0:
            y = moba_fox_mixer(h, ab_w_in, ab_forget_bias, ab_w_out, positions)
        else:
            y = mla_mixer(h, mla_w_dq, mla_q_norm, mla_w_uq, mla_w_dkv, mla_kv_norm,
                          mla_w_ukv, mla_w_out, positions)
        x = x + g_m * y
        h = rms_norm(x, norm_ffn[layer]) * (1 + sc_f) + sh_f
        x = x + g_f * moe_ffn(h, router_w, router_bias, exp_w_gate[layer],
                              exp_w_up[layer], exp_w_down[layer])
    return rms_norm(x, final_norm)
```

```python
import functools
import math

import jax
import jax.numpy as jnp
from jax import lax
from jax.experimental import pallas as pl
from jax.experimental.pallas import tpu as pltpu

F32 = jnp.float32
BF16 = jnp.bfloat16

HEAD_DIM = 128
MOBA_HEADS = 8
FOX_HEADS = 8
ROT_DIM = HEAD_DIM // 4
ROPE_THETA = 500000.0
MOBA_BLOCK = 256
MOBA_TOPK = 3
MLA_HEADS = 16
MLA_Q_LORA = 512
MLA_KV_LORA = 512
MLA_NOPE_DIM = 128
MLA_ROPE_DIM = 64
MLA_V_DIM = 128
N_EXPERTS = 16
N_GROUPS = 4
EXPERTS_PER_GROUP = N_EXPERTS // N_GROUPS
TOP_K = 2
NORM_EPS = 1e-6

LANES = 128
V7X_VMEM_BYTES = 64 * 1024 * 1024
MIB = 1024 * 1024
LOG2E = math.log2(math.e)

NT_DIMS = (((1,), (1,)), ((), ()))


def _cparams(semantics, vmem_mib):
    assert vmem_mib * MIB < V7X_VMEM_BYTES
    return pltpu.CompilerParams(dimension_semantics=semantics,
                                vmem_limit_bytes=vmem_mib * MIB)


def _pick(n, pref):
    t = min(pref, n)
    while n % t:
        t //= 2
    return t


def _adaln_kernel(c_ref, w_ref, b_ref, o_ref):
    c = c_ref[...]
    ca = (c * jax.nn.sigmoid(c)).astype(BF16)
    acc = jnp.dot(ca, w_ref[0].astype(BF16), preferred_element_type=F32)
    o_ref[0] = acc + b_ref[0]


def _adaln(c, ada_w, ada_b):
    depth, d, n6 = ada_w.shape
    b = c.shape[0]
    rows = 8
    c_pad = jnp.pad(c, ((0, rows - b), (0, 0)))
    tn = _pick(n6, 1024)
    out = pl.pallas_call(
        _adaln_kernel,
        out_shape=jax.ShapeDtypeStruct((depth, rows, n6), F32),
        grid=(depth, n6 // tn),
        in_specs=[
            pl.BlockSpec((rows, d), lambda l, j: (0, 0)),
            pl.BlockSpec((1, d, tn), lambda l, j: (l, 0, j)),
            pl.BlockSpec((1, 1, tn), lambda l, j: (l, 0, j)),
        ],
        out_specs=pl.BlockSpec((1, rows, tn), lambda l, j: (l, 0, j)),
        compiler_params=_cparams(("parallel", "parallel"), 40),
        name="adaln_mod",
    )(c_pad, ada_w, ada_b.reshape(depth, 1, n6))
    return out[:, :b, :].reshape(depth, b, 6, d)


NORM_ROWS = 256


def _norm_mod_rows(x_ref, gain_ref, mod_ref, shift_row, scale_row, store):
    tm = x_ref.shape[0]
    rows = min(NORM_ROWS, tm)
    gain = gain_ref[...]
    scale1p = 1.0 + mod_ref[0, scale_row:scale_row + 1, :]
    shift = mod_ref[0, shift_row:shift_row + 1, :]

    def body(c, carry):
        r0 = pl.multiple_of(c * rows, rows)
        x = x_ref[pl.ds(r0, rows), :]
        ms = jnp.mean(x * x, axis=-1, keepdims=True)
        y = x * lax.rsqrt(ms + NORM_EPS)
        h = (y * gain) * scale1p + shift
        store(r0, rows, h)
        return carry

    lax.fori_loop(0, tm // rows, body, 0)


def _rope_group(a, cos_t, sin_up, sin_dn, half):
    up = pltpu.roll(a, LANES - half, 1)
    dn = pltpu.roll(a, half, 1)
    return a * cos_t + up * sin_up + dn * sin_dn


def _rope_tables(seq, rot_dim):
    half = rot_dim // 2
    inv_freq = ROPE_THETA ** (-jnp.arange(half, dtype=F32) / half)
    ang = jnp.arange(seq, dtype=F32)[:, None] * inv_freq[None, :]
    cos, sin = jnp.cos(ang), jnp.sin(ang)
    zeros = jnp.zeros((seq, LANES - rot_dim), F32)
    zh = jnp.zeros((seq, half), F32)
    cos_t = jnp.concatenate([cos, cos, jnp.ones((seq, LANES - rot_dim), F32)], axis=1)
    sin_up = jnp.concatenate([-sin, zh, zeros], axis=1)
    sin_dn = jnp.concatenate([zh, sin, zeros], axis=1)
    return cos_t, sin_up, sin_dn


def _mm_kernel(*refs, prologue, epilogue, rot_groups, rot_tile_limit, rot_half,
               shift_row, scale_row, gate_row, has_side):
    it = iter(refs)
    a_ref = next(it)
    if prologue == "norm":
        gain_ref = next(it)
    if prologue == "norm" or epilogue == "residual":
        mod_ref = next(it)
    w_ref = next(it)
    if has_side:
        wside_ref = next(it)
    if epilogue == "rot":
        cos_ref, up_ref, dn_ref = next(it), next(it), next(it)
    if epilogue == "residual":
        x_ref = next(it)
    o_ref = next(it)
    if has_side:
        side_ref = next(it)
    if prologue == "norm":
        hb_ref = next(it)

    j = pl.program_id(1)

    if prologue == "norm":
        @pl.when(j == 0)
        def _():
            def store(r0, rows, h):
                hb_ref[pl.ds(r0, rows), :] = h.astype(BF16)
            _norm_mod_rows(a_ref, gain_ref, mod_ref, shift_row, scale_row, store)
            if has_side:
                side_ref[...] = jnp.dot(hb_ref[...], wside_ref[...],
                                        preferred_element_type=F32)
        lhs = hb_ref[...]
    else:
        lhs = a_ref[...]

    acc = jnp.dot(lhs, w_ref[...], preferred_element_type=F32)

    if epilogue == "rot":
        def rotated():
            groups = []
            for g in range(acc.shape[1] // LANES):
                blk = acc[:, g * LANES:(g + 1) * LANES]
                if g in rot_groups:
                    blk = _rope_group(blk, cos_ref[...], up_ref[...], dn_ref[...], rot_half)
                groups.append(blk)
            return jnp.concatenate(groups, axis=1)

        if rot_tile_limit is None:
            o_ref[...] = rotated().astype(o_ref.dtype)
        else:
            @pl.when(j < rot_tile_limit)
            def _():
                o_ref[...] = rotated().astype(o_ref.dtype)

            @pl.when(j >= rot_tile_limit)
            def _():
                o_ref[...] = acc.astype(o_ref.dtype)
    elif epilogue == "residual":
        gate = mod_ref[0, gate_row:gate_row + 1, :]
        o_ref[...] = x_ref[...] + gate * acc
    else:
        o_ref[...] = acc.astype(o_ref.dtype)


def _matmul(a, w, *, seq, out_dtype=BF16, tm=1024, tn=512, prologue="none", gain=None,
            mod=None, shift_row=0, scale_row=1, epilogue="none", rot_tables=None,
            rot_groups=(), rot_tile_limit=None, rot_half=0, x_res=None, gate_row=2,
            w_side=None, name="matmul"):
    m, k = a.shape
    n = w.shape[1]
    tm = _pick(seq, tm)
    tn = _pick(n, tn)
    tiles_per_seq = seq // tm
    has_side = w_side is not None
    in_specs = [pl.BlockSpec((tm, k), lambda i, j: (i, 0))]
    args = [a]
    if prologue == "norm":
        in_specs.append(pl.BlockSpec((1, k), lambda i, j: (0, 0)))
        args.append(gain.reshape(1, k))
    if prologue == "norm":
        in_specs.append(pl.BlockSpec((1, 6, k), lambda i, j: (i // tiles_per_seq, 0, 0)))
        args.append(mod)
    elif epilogue == "residual":
        in_specs.append(pl.BlockSpec((1, 6, tn), lambda i, j: (i // tiles_per_seq, 0, j)))
        args.append(mod)
    in_specs.append(pl.BlockSpec((k, tn), lambda i, j: (0, j)))
    args.append(w)
    if has_side:
        in_specs.append(pl.BlockSpec((k, LANES), lambda i, j: (0, 0)))
        args.append(w_side)
    if epilogue == "rot":
        for t in rot_tables:
            in_specs.append(pl.BlockSpec((tm, LANES), lambda i, j: (i % tiles_per_seq, 0)))
            args.append(t)
    if epilogue == "residual":
        in_specs.append(pl.BlockSpec((tm, tn), lambda i, j: (i, j)))
        args.append(x_res)
    out_shape = [jax.ShapeDtypeStruct((m, n), out_dtype)]
    out_specs = [pl.BlockSpec((tm, tn), lambda i, j: (i, j))]
    if has_side:
        out_shape.append(jax.ShapeDtypeStruct((m, LANES), F32))
        out_specs.append(pl.BlockSpec((tm, LANES), lambda i, j: (i, 0)))
    scratch = [pltpu.VMEM((tm, k), BF16)] if prologue == "norm" else []
    kernel = functools.partial(
        _mm_kernel, prologue=prologue, epilogue=epilogue, rot_groups=tuple(rot_groups),
        rot_tile_limit=rot_tile_limit, rot_half=rot_half, shift_row=shift_row,
        scale_row=scale_row, gate_row=gate_row, has_side=has_side)
    outs = pl.pallas_call(
        kernel,
        out_shape=out_shape,
        grid=(m // tm, n // tn),
        in_specs=in_specs,
        out_specs=out_specs,
        scratch_shapes=scratch,
        compiler_params=_cparams(("parallel", "arbitrary"), 48),
        name=name,
    )(*args)
    return outs if has_side else outs[0]


def _fox_prep_kernel(flog_ref, bias_ref, qx_ref, kx_ref, carry_ref):
    tc = flog_ref.shape[0]

    @pl.when(pl.program_id(1) == 0)
    def _():
        carry_ref[...] = jnp.zeros_like(carry_ref)

    z = flog_ref[...] + bias_ref[...]
    logf = jnp.minimum(z, 0.0) - jnp.log1p(jnp.exp(-jnp.abs(z)))
    row = lax.broadcasted_iota(jnp.int32, (tc, tc), 0)
    col = lax.broadcasted_iota(jnp.int32, (tc, tc), 1)
    tri = jnp.where(col <= row, 1.0, 0.0).astype(F32)
    csum = jnp.dot(tri, logf, preferred_element_type=F32,
                   precision=lax.Precision.HIGHEST) + carry_ref[...]
    carry_ref[...] = csum[tc - 1:tc, :]
    csum = csum * LOG2E
    lane = lax.broadcasted_iota(jnp.int32, (tc, LANES), 1)
    one = jnp.ones((tc, LANES), F32)
    zero = jnp.zeros((tc, LANES), F32)
    for h in range(FOX_HEADS):
        colv = jnp.broadcast_to(csum[:, h:h + 1], (tc, LANES))
        hi = colv.astype(BF16).astype(F32)
        r1 = colv - hi
        mid = r1.astype(BF16).astype(F32)
        lo = r1 - mid
        qv = jnp.where(lane == 0, hi, jnp.where(lane == 1, mid, jnp.where(
            lane == 2, lo, jnp.where(lane < 6, one, zero))))
        kv = jnp.where(lane < 3, one, jnp.where(lane == 3, -hi, jnp.where(
            lane == 4, -mid, jnp.where(lane == 5, -lo, zero))))
        qx_ref[:, h * LANES:(h + 1) * LANES] = qv.astype(BF16)
        kx_ref[:, h * LANES:(h + 1) * LANES] = kv.astype(BF16)


def _fox_prep(flog, forget_bias, batch, seq):
    n = flog.shape[0]
    tc = _pick(seq, 256)
    bias = jnp.pad(forget_bias, (0, LANES - FOX_HEADS)).reshape(1, LANES)
    width = FOX_HEADS * LANES
    steps = seq // tc
    return pl.pallas_call(
        _fox_prep_kernel,
        out_shape=[jax.ShapeDtypeStruct((n, width), BF16)] * 2,
        grid=(batch, steps),
        in_specs=[pl.BlockSpec((tc, LANES), lambda b, s: (b * steps + s, 0)),
                  pl.BlockSpec((1, LANES), lambda b, s: (0, 0))],
        out_specs=[pl.BlockSpec((tc, width), lambda b, s: (b * steps + s, 0))] * 2,
        scratch_shapes=[pltpu.VMEM((1, LANES), F32)],
        compiler_params=_cparams(("parallel", "arbitrary"), 32),
        name="fox_prep",
    )(flog, bias)


def _attn_kernel(*refs, mode, tq):
    it = iter(refs)
    q_ref, k_ref, v_ref = next(it), next(it), next(it)
    if mode == "fox":
        qx_ref, kx_ref = next(it), next(it)
    o_ref = next(it)
    m_ref, l_ref, acc_ref = next(it), next(it), next(it)
    if mode == "moba":
        kmean_ref, sel_ref = next(it), next(it)

    i = pl.program_id(2)
    neg_inf = jnp.float32(-jnp.inf)

    q = q_ref[...]
    if mode == "fox":
        q = jnp.concatenate([q, qx_ref[...]], axis=1)

    def scores(j):
        r0 = pl.multiple_of(j * tq, tq)
        k = k_ref[pl.ds(r0, tq), :]
        if mode == "fox":
            k = jnp.concatenate([k, kx_ref[pl.ds(r0, tq), :]], axis=1)
        s = lax.dot_general(q, k, NT_DIMS, preferred_element_type=F32)
        return s, v_ref[pl.ds(r0, tq), :]

    if mode == "moba":
        nb = k_ref.shape[0] // MOBA_BLOCK

        @pl.when(i == 0)
        def _():
            kmean_ref[...] = jnp.zeros_like(kmean_ref)
            for b in range(nb):
                kb = k_ref[b * MOBA_BLOCK:(b + 1) * MOBA_BLOCK, :].astype(F32)
                kmean_ref[b:b + 1, :] = jnp.mean(kb, axis=0, keepdims=True)

        gate = lax.dot_general(q.astype(F32), kmean_ref[...], NT_DIMS,
                               preferred_element_type=F32, precision=lax.Precision.HIGHEST)
        lane = lax.broadcasted_iota(jnp.int32, gate.shape, 1)
        lane_f = lane.astype(F32)
        past = lane < i
        g = jnp.where(past, gate, neg_inf)
        chosen = jnp.zeros(gate.shape, F32)
        for _ in range(min(MOBA_TOPK, nb - 1)):
            mx = jnp.max(g, axis=1, keepdims=True)
            first = jnp.min(jnp.where(g == mx, lane_f, float(LANES)), axis=1, keepdims=True)
            pick = lane_f == first
            chosen = jnp.where(jnp.logical_and(pick, past), 1.0, chosen)
            g = jnp.where(pick, neg_inf, g)
        sel_ref[...] = chosen

    s, v = scores(i)
    row = lax.broadcasted_iota(jnp.int32, s.shape, 0)
    col = lax.broadcasted_iota(jnp.int32, s.shape, 1)
    s = jnp.where(col <= row, s, neg_inf)
    m0 = jnp.max(s, axis=1, keepdims=True)
    p = jnp.exp2(s - m0)
    m_ref[...] = m0
    l_ref[...] = jnp.sum(p, axis=1, keepdims=True)
    acc_ref[...] = jnp.dot(p.astype(BF16), v, preferred_element_type=F32)

    def body(j, carry):
        s, v = scores(j)
        if mode == "moba":
            lane = lax.broadcasted_iota(jnp.int32, sel_ref.shape, 1)
            hit = jnp.max(jnp.where(lane == j, sel_ref[...], 0.0), axis=1, keepdims=True)
            s = jnp.where(hit > 0.0, s, neg_inf)
        m_old = m_ref[...]
        m_new = jnp.maximum(m_old, jnp.max(s, axis=1, keepdims=True))
        alpha = jnp.exp2(m_old - m_new)
        p = jnp.exp2(s - m_new)
        l_ref[...] = alpha * l_ref[...] + jnp.sum(p, axis=1, keepdims=True)
        acc_ref[...] = alpha * acc_ref[...] + jnp.dot(p.astype(BF16), v,
                                                      preferred_element_type=F32)
        m_ref[...] = m_new
        return carry

    lax.fori_loop(0, i, body, 0)
    o_ref[...] = (acc_ref[...] / l_ref[...]).astype(o_ref.dtype)


def _attention(q_arr, k_arr, v_arr, o_shape_cols, *, batch, seq, heads, dk, dv, q_col, k_col,
               v_col, o_col, mode, qx=None, kx=None, name="attn"):
    n = batch * seq
    tq = MOBA_BLOCK
    assert seq % tq == 0
    nq = seq // tq
    in_specs = [
        pl.BlockSpec((tq, dk), lambda b, h, i: (b * nq + i, q_col + h)),
        pl.BlockSpec((seq, dk), lambda b, h, i: (b, k_col + h)),
        pl.BlockSpec((seq, dv), lambda b, h, i: (b, v_col + h)),
    ]
    args = [q_arr, k_arr, v_arr]
    if mode == "fox":
        in_specs += [pl.BlockSpec((tq, LANES), lambda b, h, i: (b * nq + i, h)),
                     pl.BlockSpec((seq, LANES), lambda b, h, i: (b, h))]
        args += [qx, kx]
    scratch = [pltpu.VMEM((tq, 1), F32), pltpu.VMEM((tq, 1), F32), pltpu.VMEM((tq, dv), F32)]
    if mode == "moba":
        scratch += [pltpu.VMEM((LANES, dk), F32), pltpu.VMEM((tq, LANES), F32)]
    return pl.pallas_call(
        functools.partial(_attn_kernel, mode=mode, tq=tq),
        out_shape=jax.ShapeDtypeStruct((n, o_shape_cols), BF16),
        grid=(batch, heads, nq),
        in_specs=in_specs,
        out_specs=pl.BlockSpec((tq, dv), lambda b, h, i: (b * nq + i, o_col + h)),
        scratch_shapes=scratch,
        compiler_params=_cparams(("parallel", "parallel", "arbitrary"), 32),
        name=name,
    )(*args)


def _top2_of4(vals):
    a, b, c, d = vals
    hi1, lo1 = jnp.maximum(a, b), jnp.minimum(a, b)
    hi2, lo2 = jnp.maximum(c, d), jnp.minimum(c, d)
    return jnp.maximum(hi1, hi2) + jnp.maximum(jnp.minimum(hi1, hi2), jnp.maximum(lo1, lo2))


def _route_kernel(x_ref, gain_ref, mod_ref, rwt_ref, rbias_ref, h_ref, idx_ref, wgt_ref):
    def store(r0, rows, h):
        h_ref[pl.ds(r0, rows), :] = h
    _norm_mod_rows(x_ref, gain_ref, mod_ref, 3, 4, store)

    logits = lax.dot_general(rwt_ref[...], h_ref[...], NT_DIMS, preferred_element_type=F32,
                             precision=lax.Precision.HIGHEST)
    score = jax.nn.sigmoid(logits)
    biased = score + rbias_ref[...]
    rows_b = [biased[e:e + 1, :] for e in range(N_EXPERTS)]
    rows_s = [score[e:e + 1, :] for e in range(N_EXPERTS)]
    grp = [_top2_of4(rows_b[g * EXPERTS_PER_GROUP:(g + 1) * EXPERTS_PER_GROUP])
           for g in range(N_GROUPS)]
    best, g_sel = grp[0], jnp.zeros_like(grp[0], dtype=jnp.int32)
    for g in range(1, N_GROUPS):
        better = grp[g] > best
        best = jnp.where(better, grp[g], best)
        g_sel = jnp.where(better, g, g_sel)

    def in_group(rows, r):
        out = rows[r]
        for g in range(1, N_GROUPS):
            out = jnp.where(g_sel == g, rows[g * EXPERTS_PER_GROUP + r], out)
        return out

    cb = [in_group(rows_b, r) for r in range(EXPERTS_PER_GROUP)]
    cs = [in_group(rows_s, r) for r in range(EXPERTS_PER_GROUP)]
    neg_inf = jnp.float32(-jnp.inf)
    picks = []
    for _ in range(TOP_K):
        bv, bi, bs = cb[0], jnp.zeros_like(g_sel), cs[0]
        for r in range(1, EXPERTS_PER_GROUP):
            better = cb[r] > bv
            bv = jnp.where(better, cb[r], bv)
            bi = jnp.where(better, r, bi)
            bs = jnp.where(better, cs[r], bs)
        picks.append((bi, bs))
        cb = [jnp.where(bi == r, neg_inf, cb[r]) for r in range(EXPERTS_PER_GROUP)]
    (i0, s0), (i1, s1) = picks
    total = s0 + s1
    idx_ref[0:1, :] = g_sel * EXPERTS_PER_GROUP + i0
    idx_ref[1:2, :] = g_sel * EXPERTS_PER_GROUP + i1
    wgt_ref[0:1, :] = s0 / total
    wgt_ref[1:2, :] = s1 / total


def _route(x, gain, mod, router_w, router_bias, seq):
    n, d = x.shape
    tm = _pick(seq, 512)
    tiles_per_seq = seq // tm
    return pl.pallas_call(
        _route_kernel,
        out_shape=[jax.ShapeDtypeStruct((n, d), F32),
                   jax.ShapeDtypeStruct((TOP_K, n), jnp.int32),
                   jax.ShapeDtypeStruct((TOP_K, n), F32)],
        grid=(n // tm,),
        in_specs=[pl.BlockSpec((tm, d), lambda i: (i, 0)),
                  pl.BlockSpec((1, d), lambda i: (0, 0)),
                  pl.BlockSpec((1, 6, d), lambda i: (i // tiles_per_seq, 0, 0)),
                  pl.BlockSpec((N_EXPERTS, d), lambda i: (0, 0)),
                  pl.BlockSpec((N_EXPERTS, 1), lambda i: (0, 0))],
        out_specs=[pl.BlockSpec((tm, d), lambda i: (i, 0)),
                   pl.BlockSpec((TOP_K, tm), lambda i: (0, i)),
                   pl.BlockSpec((TOP_K, tm), lambda i: (0, i))],
        compiler_params=_cparams(("parallel",), 32),
        name="moe_route",
    )(x, gain.reshape(1, d), mod, router_w.T, router_bias.reshape(N_EXPERTS, 1))


GATHER_ROWS = 512


def _gather_kernel(src_ref, table_ref, out_ref, sem):
    base = pl.program_id(0) * GATHER_ROWS

    def row_copy(r, src_row):
        return pltpu.make_async_copy(table_ref.at[pl.ds(src_row, 1), :],
                                     out_ref.at[pl.ds(base + r, 1), :], sem)

    def start(r, carry):
        row_copy(r, src_ref[base + r]).start()
        return carry

    def wait(r, carry):
        row_copy(r, 0).wait()
        return carry

    lax.fori_loop(0, GATHER_ROWS, start, 0)
    lax.fori_loop(0, GATHER_ROWS, wait, 0)


def _gather_rows(src, table):
    p = src.shape[0]
    assert p % GATHER_ROWS == 0
    return pl.pallas_call(
        _gather_kernel,
        out_shape=jax.ShapeDtypeStruct((p, table.shape[1]), table.dtype),
        grid_spec=pltpu.PrefetchScalarGridSpec(
            num_scalar_prefetch=1,
            grid=(p // GATHER_ROWS,),
            in_specs=[pl.BlockSpec(memory_space=pl.ANY)],
            out_specs=pl.BlockSpec(memory_space=pl.ANY),
            scratch_shapes=[pltpu.SemaphoreType.DMA],
        ),
        compiler_params=_cparams(("arbitrary",), 16),
        name="row_gather",
    )(src, table)


GMM_ROWS = 512
GMM_COLS = 128


def _gmm_kernel(eid_ref, nvalid_ref, hs_ref, wg_ref, wu_ref, wd_ref, wrow_ref, y_ref, acc_ref,
                hb_ref):
    t, f = pl.program_id(0), pl.program_id(1)
    nf = pl.num_programs(1)
    valid = t < nvalid_ref[0]

    @pl.when(jnp.logical_and(valid, f == 0))
    def _():
        acc_ref[...] = jnp.zeros_like(acc_ref)
        hb_ref[...] = hs_ref[...].astype(BF16)

    @pl.when(valid)
    def _():
        hs = hb_ref[...]
        wgu = jnp.concatenate([wg_ref[0].astype(BF16), wu_ref[0].astype(BF16)], axis=1)
        gu = jnp.dot(hs, wgu, preferred_element_type=F32)
        g, u = gu[:, :GMM_COLS], gu[:, GMM_COLS:]
        act = (g * jax.nn.sigmoid(g)) * u
        acc_ref[...] += jnp.dot(act.astype(BF16), wd_ref[0].astype(BF16),
                                preferred_element_type=F32)

    @pl.when(f == nf - 1)
    def _():
        y_ref[...] = jnp.where(valid, acc_ref[...] * wrow_ref[...], 0.0)


def _gmm(eid, nvalid, hs, w_gate, w_up, w_down, wrow):
    p, d = hs.shape
    n_exp, _, d_exp = w_gate.shape
    nf = d_exp // GMM_COLS
    assert d_exp % GMM_COLS == 0 and p % GMM_ROWS == 0

    def row_tile(t, f, eid, nv):
        return jnp.minimum(t, nv[0] - 1)

    def col_tile(t, f, eid, nv):
        return jnp.where(t < nv[0], f, nf - 1)

    return pl.pallas_call(
        _gmm_kernel,
        out_shape=jax.ShapeDtypeStruct((p, d), F32),
        grid_spec=pltpu.PrefetchScalarGridSpec(
            num_scalar_prefetch=2,
            grid=(p // GMM_ROWS, nf),
            in_specs=[
                pl.BlockSpec((GMM_ROWS, d), lambda t, f, e, nv: (row_tile(t, f, e, nv), 0)),
                pl.BlockSpec((1, d, GMM_COLS),
                             lambda t, f, e, nv: (e[t], 0, col_tile(t, f, e, nv))),
                pl.BlockSpec((1, d, GMM_COLS),
                             lambda t, f, e, nv: (e[t], 0, col_tile(t, f, e, nv))),
                pl.BlockSpec((1, GMM_COLS, d),
                             lambda t, f, e, nv: (e[t], col_tile(t, f, e, nv), 0)),
                pl.BlockSpec((GMM_ROWS, 1), lambda t, f, e, nv: (t, 0)),
            ],
            out_specs=pl.BlockSpec((GMM_ROWS, d), lambda t, f, e, nv: (t, 0)),
            scratch_shapes=[pltpu.VMEM((GMM_ROWS, d), F32), pltpu.VMEM((GMM_ROWS, d), BF16)],
        ),
        compiler_params=_cparams(("arbitrary", "arbitrary"), 40),
        name="moe_gmm",
    )(eid, nvalid, hs, w_gate, w_up, w_down, wrow)


def _combine_kernel(x_ref, y_ref, mod_ref, *rest, final):
    if final:
        gain_ref, o_ref = rest
    else:
        (o_ref,) = rest
    d = x_ref.shape[1]
    rows = min(NORM_ROWS, x_ref.shape[0])
    gate = mod_ref[0, 5:6, :]

    def body(c, carry):
        r0 = pl.multiple_of(c * rows, rows)
        y = y_ref[pl.ds(r0, rows), :]
        out = x_ref[pl.ds(r0, rows), :] + gate * (y[:, :d] + y[:, d:])
        if final:
            ms = jnp.mean(out * out, axis=-1, keepdims=True)
            out = (out * lax.rsqrt(ms + NORM_EPS)) * gain_ref[...]
        o_ref[pl.ds(r0, rows), :] = out
        return carry

    lax.fori_loop(0, x_ref.shape[0] // rows, body, 0)


def _combine(x, y_pairs, mod, seq, final_gain=None):
    n, d = x.shape
    tm = _pick(seq, 256)
    tiles_per_seq = seq // tm
    final = final_gain is not None
    in_specs = [pl.BlockSpec((tm, d), lambda i: (i, 0)),
                pl.BlockSpec((tm, TOP_K * d), lambda i: (i, 0)),
                pl.BlockSpec((1, 6, d), lambda i: (i // tiles_per_seq, 0, 0))]
    args = [x, y_pairs, mod]
    if final:
        in_specs.append(pl.BlockSpec((1, d), lambda i: (0, 0)))
        args.append(final_gain.reshape(1, d))
    return pl.pallas_call(
        functools.partial(_combine_kernel, final=final),
        out_shape=jax.ShapeDtypeStruct((n, d), F32),
        grid=(n // tm,),
        in_specs=in_specs,
        out_specs=pl.BlockSpec((tm, d), lambda i: (i, 0)),
        compiler_params=_cparams(("parallel",), 40),
        name="moe_combine",
    )(*args)


def _dispatch_plan(idx, wgt):
    n = idx.shape[1]
    slots = TOP_K * n
    cap = slots + N_EXPERTS * GMM_ROWS
    cap = -(-cap // max(GMM_ROWS, GATHER_ROWS)) * max(GMM_ROWS, GATHER_ROWS)
    expert = idx.T.reshape(slots)
    onehot = (expert[:, None] == jnp.arange(N_EXPERTS)[None, :]).astype(jnp.int32)
    csum = jnp.cumsum(onehot, axis=0)
    rank = jnp.sum(csum * onehot, axis=1) - 1
    counts = csum[-1]
    padded = -(-counts // GMM_ROWS) * GMM_ROWS
    ends = jnp.cumsum(padded)
    pos = (ends - padded)[expert] + rank
    token = jnp.arange(slots, dtype=jnp.int32) // TOP_K
    src = jnp.zeros((cap,), jnp.int32).at[pos].set(token)
    wrow = jnp.zeros((cap,), F32).at[pos].set(wgt.T.reshape(slots))
    nvalid = (ends[-1] // GMM_ROWS).astype(jnp.int32)
    tile_start = jnp.arange(cap // GMM_ROWS, dtype=jnp.int32) * GMM_ROWS
    tile_start = jnp.minimum(tile_start, ends[-1] - GMM_ROWS)
    eid = jnp.sum(tile_start[:, None] >= ends[None, :], axis=1).astype(jnp.int32)
    return src, wrow.reshape(cap, 1), pos.astype(jnp.int32), eid, nvalid.reshape(1)


def _moe_layer(x, gain, mod, router_w, router_bias, w_gate, w_up, w_down, seq, final_gain):
    h, idx, wgt = _route(x, gain, mod, router_w, router_bias, seq)
    src, wrow, pos, eid, nvalid = _dispatch_plan(idx, wgt)
    hs = _gather_rows(src, h)
    y = _gmm(eid, nvalid, hs, w_gate, w_up, w_down, wrow)
    y_slots = _gather_rows(pos, y)
    return _combine(x, y_slots.reshape(x.shape[0], TOP_K * x.shape[1]), mod, seq, final_gain)


def _moba_fox_mixer(x, gain, mod, w_in, forget_bias, w_out, batch, seq):
    n, d = x.shape
    mix = (MOBA_HEADS + FOX_HEADS) * HEAD_DIM
    qk_scale = HEAD_DIM ** -0.5 * LOG2E
    s1 = MOBA_HEADS * HEAD_DIM
    s2 = FOX_HEADS * HEAD_DIM
    col_scale = jnp.ones((3 * mix,), F32)
    col_scale = col_scale.at[:s1].set(qk_scale).at[3 * s1:3 * s1 + s2].set(qk_scale)
    w_main = (w_in[:, :3 * mix] * col_scale[None, :]).astype(BF16)
    w_forget = jnp.pad(w_in[:, 3 * mix:], ((0, 0), (0, LANES - FOX_HEADS))).astype(BF16)
    tn = 512
    proj, flog = _matmul(
        x, w_main, seq=seq, prologue="norm", gain=gain, mod=mod, shift_row=0, scale_row=1,
        epilogue="rot", rot_tables=_rope_tables(seq, ROT_DIM), rot_groups=range(tn // LANES),
        rot_tile_limit=2 * s1 // tn, rot_half=ROT_DIM // 2, w_side=w_forget, tn=tn,
        name="l0_in_proj")
    qx, kx = _fox_prep(flog, forget_bias, batch, seq)
    o_moba = _attention(proj, proj, proj, s1, batch=batch, seq=seq, heads=MOBA_HEADS,
                        dk=HEAD_DIM, dv=HEAD_DIM, q_col=0, k_col=MOBA_HEADS,
                        v_col=2 * MOBA_HEADS, o_col=0, mode="moba", name="moba_attn")
    base = 3 * MOBA_HEADS
    o_fox = _attention(proj, proj, proj, s2, batch=batch, seq=seq, heads=FOX_HEADS,
                       dk=HEAD_DIM, dv=HEAD_DIM, q_col=base, k_col=base + FOX_HEADS,
                       v_col=base + 2 * FOX_HEADS, o_col=0, mode="fox", qx=qx, kx=kx,
                       name="fox_attn")
    o = jnp.concatenate([o_moba, o_fox], axis=1)
    return _matmul(o, w_out.astype(BF16), seq=seq, out_dtype=F32, epilogue="residual",
                   mod=mod, x_res=x, gate_row=2, name="l0_out_proj")


def _mla_down_kernel(x_ref, gain_ref, mod_ref, wd_ref, qn_ref, kvn_ref, cos_ref, up_ref, dn_ref,
                     cq_ref, kin_ref, hb_ref):
    def store(r0, rows, h):
        hb_ref[pl.ds(r0, rows), :] = h.astype(BF16)
    _norm_mod_rows(x_ref, gain_ref, mod_ref, 0, 1, store)
    a = jnp.dot(hb_ref[...], wd_ref[...], preferred_element_type=F32)

    def rms(v, g):
        ms = jnp.mean(v * v, axis=-1, keepdims=True)
        return (v * lax.rsqrt(ms + NORM_EPS)) * g

    q_end = MLA_Q_LORA
    kv_end = MLA_Q_LORA + MLA_KV_LORA
    cq_ref[...] = rms(a[:, :q_end], qn_ref[...]).astype(BF16)
    kin_ref[:, :MLA_KV_LORA] = rms(a[:, q_end:kv_end], kvn_ref[...]).astype(BF16)
    kpe = _rope_group(a[:, kv_end:kv_end + LANES], cos_ref[...], up_ref[...], dn_ref[...],
                      MLA_ROPE_DIM // 2)
    kin_ref[:, MLA_KV_LORA:] = kpe.astype(BF16)


def _mla_down(x, gain, mod, w_down, q_norm, kv_norm, tables, seq):
    n, d = x.shape
    tm = _pick(seq, 512)
    tiles_per_seq = seq // tm
    wcols = w_down.shape[1]
    const = lambda i: (0, 0)
    return pl.pallas_call(
        _mla_down_kernel,
        out_shape=[jax.ShapeDtypeStruct((n, MLA_Q_LORA), BF16),
                   jax.ShapeDtypeStruct((n, MLA_KV_LORA + LANES), BF16)],
        grid=(n // tm,),
        in_specs=[pl.BlockSpec((tm, d), lambda i: (i, 0)),
                  pl.BlockSpec((1, d), const),
                  pl.BlockSpec((1, 6, d), lambda i: (i // tiles_per_seq, 0, 0)),
                  pl.BlockSpec((d, wcols), const),
                  pl.BlockSpec((1, MLA_Q_LORA), const),
                  pl.BlockSpec((1, MLA_KV_LORA), const)]
                 + [pl.BlockSpec((tm, LANES), lambda i: (i % tiles_per_seq, 0))] * 3,
        out_specs=[pl.BlockSpec((tm, MLA_Q_LORA), lambda i: (i, 0)),
                   pl.BlockSpec((tm, MLA_KV_LORA + LANES), lambda i: (i, 0))],
        scratch_shapes=[pltpu.VMEM((tm, d), BF16)],
        compiler_params=_cparams(("parallel",), 40),
        name="mla_down",
    )(x, gain.reshape(1, d), mod, w_down, q_norm.reshape(1, -1), kv_norm.reshape(1, -1), *tables)


def _mla_mixer(x, gain, mod, w_dq, q_norm, w_uq, w_dkv, kv_norm, w_ukv, w_out, batch, seq):
    n, d = x.shape
    dk = 2 * LANES
    pad_rope = LANES - MLA_ROPE_DIM
    qk_scale = (MLA_NOPE_DIM + MLA_ROPE_DIM) ** -0.5 * LOG2E
    tables = _rope_tables(seq, MLA_ROPE_DIM)
    w_down = jnp.concatenate(
        [w_dq, w_dkv, jnp.zeros((d, pad_rope), F32)], axis=1).astype(BF16)
    cq, kin = _mla_down(x, gain, mod, w_down, q_norm, kv_norm, tables, seq)

    w_q = (w_uq * qk_scale).reshape(MLA_Q_LORA, MLA_HEADS, MLA_NOPE_DIM + MLA_ROPE_DIM)
    w_q = jnp.pad(w_q, ((0, 0), (0, 0), (0, pad_rope))).reshape(MLA_Q_LORA, MLA_HEADS * dk)
    tn = 512
    q = _matmul(cq, w_q.astype(BF16), seq=seq, epilogue="rot", rot_tables=tables,
                rot_groups=range(1, tn // LANES, 2), rot_half=MLA_ROPE_DIM // 2, tn=tn,
                name="mla_q_proj")

    w_kv = w_ukv.reshape(MLA_KV_LORA, MLA_HEADS, MLA_NOPE_DIM + MLA_V_DIM)
    w_knope = jnp.pad(w_kv[:, :, :MLA_NOPE_DIM], ((0, LANES), (0, 0), (0, LANES)))
    rope_place = jnp.eye(LANES, dk, k=LANES, dtype=F32) * (
        jnp.arange(LANES) < MLA_ROPE_DIM).astype(F32)[:, None]
    rope_rows = jnp.concatenate([jnp.zeros((MLA_KV_LORA, dk), F32), rope_place], axis=0)
    w_k = w_knope + rope_rows[:, None, :]
    w_v = jnp.pad(w_kv[:, :, MLA_NOPE_DIM:], ((0, LANES), (0, 0), (0, 0)))
    w_kv_all = jnp.concatenate([w_k.reshape(MLA_KV_LORA + LANES, MLA_HEADS * dk),
                                w_v.reshape(MLA_KV_LORA + LANES, MLA_HEADS * MLA_V_DIM)], axis=1)
    kv = _matmul(kin, w_kv_all.astype(BF16), seq=seq, tn=tn, name="mla_kv_proj")

    o = _attention(q, kv, kv, MLA_HEADS * MLA_V_DIM, batch=batch, seq=seq, heads=MLA_HEADS,
                   dk=dk, dv=MLA_V_DIM, q_col=0, k_col=0, v_col=MLA_HEADS * dk // MLA_V_DIM,
                   o_col=0, mode="plain", name="mla_attn")
    return _matmul(o, w_out.astype(BF16), seq=seq, out_dtype=F32, epilogue="residual",
                   mod=mod, x_res=x, gate_row=2, name="l1_out_proj")


def kernel(x, c, ada_w, ada_b, norm_mix, norm_ffn, ab_w_in, ab_forget_bias, ab_w_out, mla_w_dq,
           mla_q_norm, mla_w_uq, mla_w_dkv, mla_kv_norm, mla_w_ukv, mla_w_out, router_w,
           router_bias, exp_w_gate, exp_w_up, exp_w_down, final_norm):
    batch, seq, d = x.shape
    depth = ada_w.shape[0]
    mod = _adaln(c, ada_w, ada_b)
    xs = x.reshape(batch * seq, d)
    for layer in range(depth):
        if layer % 2 == 0:
            xs = _moba_fox_mixer(xs, norm_mix[layer], mod[layer], ab_w_in, ab_forget_bias,
                                 ab_w_out, batch, seq)
        else:
            xs = _mla_mixer(xs, norm_mix[layer], mod[layer], mla_w_dq, mla_q_norm, mla_w_uq,
                            mla_w_dkv, mla_kv_norm, mla_w_ukv, mla_w_out, batch, seq)
        final_gain = final_norm if layer == depth - 1 else None
        xs = _moe_layer(xs, norm_ffn[layer], mod[layer], router_w, router_bias,
                        exp_w_gate[layer], exp_w_up[layer], exp_w_down[layer], seq, final_gain)
    return xs.reshape(batch, seq, d)
```

```python
import functools
import math

import jax
import jax.numpy as jnp
from jax import lax
from jax.experimental import pallas as pl
from jax.experimental.pallas import tpu as pltpu

F32 = jnp.float32
BF16 = jnp.bfloat16

HEAD_DIM = 128
MOBA_HEADS = 8
FOX_HEADS = 8
ROT_DIM = HEAD_DIM // 4
ROPE_THETA = 500000.0
MOBA_BLOCK = 256
MOBA_TOPK = 3
MLA_HEADS = 16
MLA_Q_LORA = 512
MLA_KV_LORA = 512
MLA_NOPE_DIM = 128
MLA_ROPE_DIM = 64
MLA_V_DIM = 128
N_EXPERTS = 16
N_GROUPS = 4
EXPERTS_PER_GROUP = N_EXPERTS // N_GROUPS
TOP_K = 2
NORM_EPS = 1e-6

LANES = 128
V7X_VMEM_BYTES = 64 * 1024 * 1024
MIB = 1024 * 1024
LOG2E = math.log2(math.e)

NT_DIMS = (((1,), (1,)), ((), ()))


def _cparams(semantics, vmem_mib):
    assert vmem_mib * MIB < V7X_VMEM_BYTES
    return pltpu.CompilerParams(dimension_semantics=semantics,
                                vmem_limit_bytes=vmem_mib * MIB)


def _pick(n, pref):
    t = min(pref, n)
    while n % t:
        t //= 2
    return t


def _adaln_kernel(c_ref, w_ref, b_ref, o_ref):
    c = c_ref[...]
    ca = (c * jax.nn.sigmoid(c)).astype(BF16)
    acc = jnp.dot(ca, w_ref[0].astype(BF16), preferred_element_type=F32)
    o_ref[0] = acc + b_ref[0]


def _adaln(c, ada_w, ada_b):
    depth, d, n6 = ada_w.shape
    b = c.shape[0]
    rows = 8
    c_pad = jnp.pad(c, ((0, rows - b), (0, 0)))
    tn = _pick(n6, 1024)
    out = pl.pallas_call(
        _adaln_kernel,
        out_shape=jax.ShapeDtypeStruct((depth, rows, n6), F32),
        grid=(depth, n6 // tn),
        in_specs=[
            pl.BlockSpec((rows, d), lambda l, j: (0, 0)),
            pl.BlockSpec((1, d, tn), lambda l, j: (l, 0, j)),
            pl.BlockSpec((1, 1, tn), lambda l, j: (l, 0, j)),
        ],
        out_specs=pl.BlockSpec((1, rows, tn), lambda l, j: (l, 0, j)),
        compiler_params=_cparams(("parallel", "parallel"), 40),
        name="adaln_mod",
    )(c_pad, ada_w, ada_b.reshape(depth, 1, n6))
    return out[:, :b, :].reshape(depth, b, 6, d)


NORM_ROWS = 256


def _norm_mod_rows(x_ref, gain_ref, mod_ref, shift_row, scale_row, store):
    tm = x_ref.shape[0]
    rows = min(NORM_ROWS, tm)
    gain = gain_ref[...]
    scale1p = 1.0 + mod_ref[0, scale_row:scale_row + 1, :]
    shift = mod_ref[0, shift_row:shift_row + 1, :]

    def body(c, carry):
        r0 = pl.multiple_of(c * rows, rows)
        x = x_ref[pl.ds(r0, rows), :]
        ms = jnp.mean(x * x, axis=-1, keepdims=True)
        y = x * lax.rsqrt(ms + NORM_EPS)
        h = (y * gain) * scale1p + shift
        store(r0, rows, h)
        return carry

    lax.fori_loop(0, tm // rows, body, 0)


def _rope_group(a, cos_t, sin_up, sin_dn, half):
    up = pltpu.roll(a, LANES - half, 1)
    dn = pltpu.roll(a, half, 1)
    return a * cos_t + up * sin_up + dn * sin_dn


def _rope_tables(seq, rot_dim):
    half = rot_dim // 2
    inv_freq = ROPE_THETA ** (-jnp.arange(half, dtype=F32) / half)
    ang = jnp.arange(seq, dtype=F32)[:, None] * inv_freq[None, :]
    cos, sin = jnp.cos(ang), jnp.sin(ang)
    zeros = jnp.zeros((seq, LANES - rot_dim), F32)
    zh = jnp.zeros((seq, half), F32)
    cos_t = jnp.concatenate([cos, cos, jnp.ones((seq, LANES - rot_dim), F32)], axis=1)
    sin_up = jnp.concatenate([-sin, zh, zeros], axis=1)
    sin_dn = jnp.concatenate([zh, sin, zeros], axis=1)
    return cos_t, sin_up, sin_dn


def _mm_kernel(*refs, prologue, epilogue, rot_groups, rot_tile_limit, rot_half,
               shift_row, scale_row, gate_row, has_side):
    it = iter(refs)
    a_ref = next(it)
    if prologue == "norm":
        gain_ref = next(it)
    if prologue == "norm" or epilogue == "residual":
        mod_ref = next(it)
    w_ref = next(it)
    if has_side:
        wside_ref = next(it)
    if epilogue == "rot":
        cos_ref, up_ref, dn_ref = next(it), next(it), next(it)
    if epilogue == "residual":
        x_ref = next(it)
    o_ref = next(it)
    if has_side:
        side_ref = next(it)
    if prologue == "norm":
        hb_ref = next(it)

    j = pl.program_id(1)

    if prologue == "norm":
        @pl.when(j == 0)
        def _():
            def store(r0, rows, h):
                hb_ref[pl.ds(r0, rows), :] = h.astype(BF16)
            _norm_mod_rows(a_ref, gain_ref, mod_ref, shift_row, scale_row, store)
            if has_side:
                side_ref[...] = jnp.dot(hb_ref[...], wside_ref[...],
                                        preferred_element_type=F32)
        lhs = hb_ref[...]
    else:
        lhs = a_ref[...]

    acc = jnp.dot(lhs, w_ref[...], preferred_element_type=F32)

    if epilogue == "rot":
        def rotated():
            groups = []
            for g in range(acc.shape[1] // LANES):
                blk = acc[:, g * LANES:(g + 1) * LANES]
                if g in rot_groups:
                    blk = _rope_group(blk, cos_ref[...], up_ref[...], dn_ref[...], rot_half)
                groups.append(blk)
            return jnp.concatenate(groups, axis=1)

        if rot_tile_limit is None:
            o_ref[...] = rotated().astype(o_ref.dtype)
        else:
            @pl.when(j < rot_tile_limit)
            def _():
                o_ref[...] = rotated().astype(o_ref.dtype)

            @pl.when(j >= rot_tile_limit)
            def _():
                o_ref[...] = acc.astype(o_ref.dtype)
    elif epilogue == "residual":
        gate = mod_ref[0, gate_row:gate_row + 1, :]
        o_ref[...] = x_ref[...] + gate * acc
    else:
        o_ref[...] = acc.astype(o_ref.dtype)


def _matmul(a, w, *, seq, out_dtype=BF16, tm=1024, tn=512, prologue="none", gain=None,
            mod=None, shift_row=0, scale_row=1, epilogue="none", rot_tables=None,
            rot_groups=(), rot_tile_limit=None, rot_half=0, x_res=None, gate_row=2,
            w_side=None, name="matmul"):
    m, k = a.shape
    n = w.shape[1]
    tm = _pick(seq, tm)
    tn = _pick(n, tn)
    tiles_per_seq = seq // tm
    has_side = w_side is not None
    in_specs = [pl.BlockSpec((tm, k), lambda i, j: (i, 0))]
    args = [a]
    if prologue == "norm":
        in_specs.append(pl.BlockSpec((1, k), lambda i, j: (0, 0)))
        args.append(gain.reshape(1, k))
    if prologue == "norm":
        in_specs.append(pl.BlockSpec((1, 6, k), lambda i, j: (i // tiles_per_seq, 0, 0)))
        args.append(mod)
    elif epilogue == "residual":
        in_specs.append(pl.BlockSpec((1, 6, tn), lambda i, j: (i // tiles_per_seq, 0, j)))
        args.append(mod)
    in_specs.append(pl.BlockSpec((k, tn), lambda i, j: (0, j)))
    args.append(w)
    if has_side:
        in_specs.append(pl.BlockSpec((k, LANES), lambda i, j: (0, 0)))
        args.append(w_side)
    if epilogue == "rot":
        for t in rot_tables:
            in_specs.append(pl.BlockSpec((tm, LANES), lambda i, j: (i % tiles_per_seq, 0)))
            args.append(t)
    if epilogue == "residual":
        in_specs.append(pl.BlockSpec((tm, tn), lambda i, j: (i, j)))
        args.append(x_res)
    out_shape = [jax.ShapeDtypeStruct((m, n), out_dtype)]
    out_specs = [pl.BlockSpec((tm, tn), lambda i, j: (i, j))]
    if has_side:
        out_shape.append(jax.ShapeDtypeStruct((m, LANES), F32))
        out_specs.append(pl.BlockSpec((tm, LANES), lambda i, j: (i, 0)))
    scratch = [pltpu.VMEM((tm, k), BF16)] if prologue == "norm" else []
    kernel = functools.partial(
        _mm_kernel, prologue=prologue, epilogue=epilogue, rot_groups=tuple(rot_groups),
        rot_tile_limit=rot_tile_limit, rot_half=rot_half, shift_row=shift_row,
        scale_row=scale_row, gate_row=gate_row, has_side=has_side)
    outs = pl.pallas_call(
        kernel,
        out_shape=out_shape,
        grid=(m // tm, n // tn),
        in_specs=in_specs,
        out_specs=out_specs,
        scratch_shapes=scratch,
        compiler_params=_cparams(("parallel", "arbitrary"), 48),
        name=name,
    )(*args)
    return outs if has_side else outs[0]


def _fox_prep_kernel(flog_ref, bias_ref, qx_ref, kx_ref, carry_ref):
    tc = flog_ref.shape[0]

    @pl.when(pl.program_id(1) == 0)
    def _():
        carry_ref[...] = jnp.zeros_like(carry_ref)

    z = flog_ref[...] + bias_ref[...]
    logf = jnp.minimum(z, 0.0) - jnp.log1p(jnp.exp(-jnp.abs(z)))
    row = lax.broadcasted_iota(jnp.int32, (tc, tc), 0)
    col = lax.broadcasted_iota(jnp.int32, (tc, tc), 1)
    tri = jnp.where(col <= row, 1.0, 0.0).astype(F32)
    csum = jnp.dot(tri, logf, preferred_element_type=F32,
                   precision=lax.Precision.HIGHEST) + carry_ref[...]
    carry_ref[...] = csum[tc - 1:tc, :]
    csum = csum * LOG2E
    lane = lax.broadcasted_iota(jnp.int32, (tc, LANES), 1)
    one = jnp.ones((tc, LANES), F32)
    zero = jnp.zeros((tc, LANES), F32)
    for h in range(FOX_HEADS):
        colv = jnp.broadcast_to(csum[:, h:h + 1], (tc, LANES))
        hi = colv.astype(BF16).astype(F32)
        r1 = colv - hi
        mid = r1.astype(BF16).astype(F32)
        lo = r1 - mid
        qv = jnp.where(lane == 0, hi, jnp.where(lane == 1, mid, jnp.where(
            lane == 2, lo, jnp.where(lane < 6, one, zero))))
        kv = jnp.where(lane < 3, one, jnp.where(lane == 3, -hi, jnp.where(
            lane == 4, -mid, jnp.where(lane == 5, -lo, zero))))
        qx_ref[:, h * LANES:(h + 1) * LANES] = qv.astype(BF16)
        kx_ref[:, h * LANES:(h + 1) * LANES] = kv.astype(BF16)


def _fox_prep(flog, forget_bias, batch, seq):
    n = flog.shape[0]
    tc = _pick(seq, 256)
    bias = jnp.pad(forget_bias, (0, LANES - FOX_HEADS)).reshape(1, LANES)
    width = FOX_HEADS * LANES
    steps = seq // tc
    return pl.pallas_call(
        _fox_prep_kernel,
        out_shape=[jax.ShapeDtypeStruct((n, width), BF16)] * 2,
        grid=(batch, steps),
        in_specs=[pl.BlockSpec((tc, LANES), lambda b, s: (b * steps + s, 0)),
                  pl.BlockSpec((1, LANES), lambda b, s: (0, 0))],
        out_specs=[pl.BlockSpec((tc, width), lambda b, s: (b * steps + s, 0))] * 2,
        scratch_shapes=[pltpu.VMEM((1, LANES), F32)],
        compiler_params=_cparams(("parallel", "arbitrary"), 32),
        name="fox_prep",
    )(flog, bias)


ATTN_TILE = 512
ATTN_HEADS_PER_STEP = 2


def _attn_kernel(*refs, mode, tq, dk, dv):
    it = iter(refs)
    q_ref, k_ref, v_ref = next(it), next(it), next(it)
    if mode == "fox":
        qx_ref, kx_ref = next(it), next(it)
    o_ref = next(it)
    m_ref, l_ref, acc_ref = next(it), next(it), next(it)
    if mode == "moba":
        kmean_ref, sel_ref = next(it), next(it)

    i = pl.program_id(2)
    neg_inf = jnp.float32(-jnp.inf)
    heads = q_ref.shape[1] // dk
    blocks_per_tile = tq // MOBA_BLOCK
    block_shift = MOBA_BLOCK.bit_length() - 1

    def q_of(h):
        q = q_ref[:, h * dk:(h + 1) * dk]
        if mode == "fox":
            q = jnp.concatenate([q, qx_ref[:, h * LANES:(h + 1) * LANES]], axis=1)
        return q

    def scores(h, q, j):
        r0 = pl.multiple_of(j * tq, tq)
        k = k_ref[pl.ds(r0, tq), h * dk:(h + 1) * dk]
        if mode == "fox":
            k = jnp.concatenate([k, kx_ref[pl.ds(r0, tq), h * LANES:(h + 1) * LANES]], axis=1)
        s = lax.dot_general(q, k, NT_DIMS, preferred_element_type=F32)
        return s, v_ref[pl.ds(r0, tq), h * dv:(h + 1) * dv]

    def block_hits(h, j):
        lane = lax.broadcasted_iota(jnp.int32, (tq, LANES), 1)
        sel = sel_ref[h]
        return [jnp.max(jnp.where(lane == j * blocks_per_tile + c, sel, 0.0), axis=1,
                        keepdims=True) for c in range(blocks_per_tile)]

    if mode == "moba":
        nb = k_ref.shape[0] // MOBA_BLOCK

        @pl.when(i == 0)
        def _():
            kmean_ref[...] = jnp.zeros_like(kmean_ref)
            for h in range(heads):
                for b in range(nb):
                    kb = k_ref[b * MOBA_BLOCK:(b + 1) * MOBA_BLOCK, h * dk:(h + 1) * dk]
                    kmean_ref[h, b:b + 1, :] = jnp.mean(kb.astype(F32), axis=0, keepdims=True)

        lane = lax.broadcasted_iota(jnp.int32, (tq, LANES), 1)
        lane_f = lane.astype(F32)
        row_block = jnp.right_shift(lax.broadcasted_iota(jnp.int32, (tq, LANES), 0), block_shift)
        past = lane < i * blocks_per_tile + row_block
        for h in range(heads):
            gate = lax.dot_general(q_of(h).astype(F32), kmean_ref[h], NT_DIMS,
                                   preferred_element_type=F32,
                                   precision=lax.Precision.HIGHEST)
            g = jnp.where(past, gate, neg_inf)
            chosen = jnp.zeros(gate.shape, F32)
            for _ in range(min(MOBA_TOPK, nb - 1)):
                mx = jnp.max(g, axis=1, keepdims=True)
                first = jnp.min(jnp.where(g == mx, lane_f, float(LANES)), axis=1,
                                keepdims=True)
                pick = lane_f == first
                chosen = jnp.where(jnp.logical_and(pick, past), 1.0, chosen)
                g = jnp.where(pick, neg_inf, g)
            sel_ref[h] = chosen

    row = lax.broadcasted_iota(jnp.int32, (tq, tq), 0)
    col = lax.broadcasted_iota(jnp.int32, (tq, tq), 1)
    row_blk = jnp.right_shift(row, block_shift)
    col_blk = jnp.right_shift(col, block_shift)
    for h in range(heads):
        s, v = scores(h, q_of(h), i)
        if mode == "moba":
            own = jnp.logical_and(row_blk == col_blk, col <= row)
            weight = jnp.where(own, 1.0, 0.0)
            hits = block_hits(h, i)
            for c in range(blocks_per_tile - 1):
                earlier = jnp.logical_and(col_blk == c, row_blk > c)
                weight = jnp.where(earlier, hits[c], weight)
            allowed = weight > 0.0
        else:
            allowed = col <= row
        s = jnp.where(allowed, s, neg_inf)
        m0 = jnp.max(s, axis=1, keepdims=True)
        p = jnp.exp2(s - m0)
        m_ref[h] = m0
        l_ref[h] = jnp.sum(p, axis=1, keepdims=True)
        acc_ref[h] = jnp.dot(p.astype(BF16), v, preferred_element_type=F32)

    def body(j, carry):
        for h in range(heads):
            s, v = scores(h, q_of(h), j)
            if mode == "moba":
                hits = block_hits(h, j)
                weight = hits[blocks_per_tile - 1]
                for c in range(blocks_per_tile - 2, -1, -1):
                    weight = jnp.where(col_blk == c, hits[c], weight)
                s = jnp.where(weight > 0.0, s, neg_inf)
            m_old = m_ref[h]
            m_new = jnp.maximum(m_old, jnp.max(s, axis=1, keepdims=True))
            alpha = jnp.exp2(m_old - m_new)
            p = jnp.exp2(s - m_new)
            l_ref[h] = alpha * l_ref[h] + jnp.sum(p, axis=1, keepdims=True)
            acc_ref[h] = alpha * acc_ref[h] + jnp.dot(p.astype(BF16), v,
                                                      preferred_element_type=F32)
            m_ref[h] = m_new
        return carry

    lax.fori_loop(0, i, body, 0)
    for h in range(heads):
        o_ref[:, h * dv:(h + 1) * dv] = (acc_ref[h] / l_ref[h]).astype(o_ref.dtype)


def _attention(q_arr, k_arr, v_arr, *, batch, seq, heads, dk, dv, q_col, k_col, v_col, mode,
               qx=None, kx=None, name="attn"):
    n = batch * seq
    tq = _pick(seq, ATTN_TILE)
    g = ATTN_HEADS_PER_STEP
    assert tq % MOBA_BLOCK == 0 and heads % g == 0
    assert q_col % g == 0 and k_col % g == 0 and v_col % g == 0
    nq = seq // tq
    in_specs = [
        pl.BlockSpec((tq, g * dk), lambda b, h, i: (b * nq + i, q_col // g + h)),
        pl.BlockSpec((seq, g * dk), lambda b, h, i: (b, k_col // g + h)),
        pl.BlockSpec((seq, g * dv), lambda b, h, i: (b, v_col // g + h)),
    ]
    args = [q_arr, k_arr, v_arr]
    if mode == "fox":
        in_specs += [pl.BlockSpec((tq, g * LANES), lambda b, h, i: (b * nq + i, h)),
                     pl.BlockSpec((seq, g * LANES), lambda b, h, i: (b, h))]
        args += [qx, kx]
    scratch = [pltpu.VMEM((g, tq, 1), F32), pltpu.VMEM((g, tq, 1), F32),
               pltpu.VMEM((g, tq, dv), F32)]
    if mode == "moba":
        scratch += [pltpu.VMEM((g, LANES, dk), F32), pltpu.VMEM((g, tq, LANES), F32)]
    return pl.pallas_call(
        functools.partial(_attn_kernel, mode=mode, tq=tq, dk=dk, dv=dv),
        out_shape=jax.ShapeDtypeStruct((n, heads * dv), BF16),
        grid=(batch, heads // g, nq),
        in_specs=in_specs,
        out_specs=pl.BlockSpec((tq, g * dv), lambda b, h, i: (b * nq + i, h)),
        scratch_shapes=scratch,
        compiler_params=_cparams(("parallel", "parallel", "arbitrary"), 40),
        name=name,
    )(*args)


def _top2_of4(vals):
    a, b, c, d = vals
    hi1, lo1 = jnp.maximum(a, b), jnp.minimum(a, b)
    hi2, lo2 = jnp.maximum(c, d), jnp.minimum(c, d)
    return jnp.maximum(hi1, hi2) + jnp.maximum(jnp.minimum(hi1, hi2), jnp.maximum(lo1, lo2))


def _route_kernel(x_ref, gain_ref, mod_ref, rwt_ref, rbias_ref, h_ref, idx_ref, wgt_ref):
    def store(r0, rows, h):
        h_ref[pl.ds(r0, rows), :] = h
    _norm_mod_rows(x_ref, gain_ref, mod_ref, 3, 4, store)

    logits = lax.dot_general(rwt_ref[...], h_ref[...], NT_DIMS, preferred_element_type=F32,
                             precision=lax.Precision.HIGHEST)
    score = jax.nn.sigmoid(logits)
    biased = score + rbias_ref[...]
    rows_b = [biased[e:e + 1, :] for e in range(N_EXPERTS)]
    rows_s = [score[e:e + 1, :] for e in range(N_EXPERTS)]
    grp = [_top2_of4(rows_b[g * EXPERTS_PER_GROUP:(g + 1) * EXPERTS_PER_GROUP])
           for g in range(N_GROUPS)]
    best, g_sel = grp[0], jnp.zeros_like(grp[0], dtype=jnp.int32)
    for g in range(1, N_GROUPS):
        better = grp[g] > best
        best = jnp.where(better, grp[g], best)
        g_sel = jnp.where(better, g, g_sel)

    def in_group(rows, r):
        out = rows[r]
        for g in range(1, N_GROUPS):
            out = jnp.where(g_sel == g, rows[g * EXPERTS_PER_GROUP + r], out)
        return out

    cb = [in_group(rows_b, r) for r in range(EXPERTS_PER_GROUP)]
    cs = [in_group(rows_s, r) for r in range(EXPERTS_PER_GROUP)]
    neg_inf = jnp.float32(-jnp.inf)
    picks = []
    for _ in range(TOP_K):
        bv, bi, bs = cb[0], jnp.zeros_like(g_sel), cs[0]
        for r in range(1, EXPERTS_PER_GROUP):
            better = cb[r] > bv
            bv = jnp.where(better, cb[r], bv)
            bi = jnp.where(better, r, bi)
            bs = jnp.where(better, cs[r], bs)
        picks.append((bi, bs))
        cb = [jnp.where(bi == r, neg_inf, cb[r]) for r in range(EXPERTS_PER_GROUP)]
    (i0, s0), (i1, s1) = picks
    total = s0 + s1
    idx_ref[0:1, :] = g_sel * EXPERTS_PER_GROUP + i0
    idx_ref[1:2, :] = g_sel * EXPERTS_PER_GROUP + i1
    wgt_ref[0:1, :] = s0 / total
    wgt_ref[1:2, :] = s1 / total


def _route(x, gain, mod, router_w, router_bias, seq):
    n, d = x.shape
    tm = _pick(seq, 512)
    tiles_per_seq = seq // tm
    return pl.pallas_call(
        _route_kernel,
        out_shape=[jax.ShapeDtypeStruct((n, d), F32),
                   jax.ShapeDtypeStruct((TOP_K, n), jnp.int32),
                   jax.ShapeDtypeStruct((TOP_K, n), F32)],
        grid=(n // tm,),
        in_specs=[pl.BlockSpec((tm, d), lambda i: (i, 0)),
                  pl.BlockSpec((1, d), lambda i: (0, 0)),
                  pl.BlockSpec((1, 6, d), lambda i: (i // tiles_per_seq, 0, 0)),
                  pl.BlockSpec((N_EXPERTS, d), lambda i: (0, 0)),
                  pl.BlockSpec((N_EXPERTS, 1), lambda i: (0, 0))],
        out_specs=[pl.BlockSpec((tm, d), lambda i: (i, 0)),
                   pl.BlockSpec((TOP_K, tm), lambda i: (0, i)),
                   pl.BlockSpec((TOP_K, tm), lambda i: (0, i))],
        compiler_params=_cparams(("parallel",), 32),
        name="moe_route",
    )(x, gain.reshape(1, d), mod, router_w.T, router_bias.reshape(N_EXPERTS, 1))


GMM_ROWS = 512
GMM_UP_COLS = 128
GMM_DOWN_COLS = 512
DMA_UNROLL = 8


def _gmm_kernel(eid_ref, nvalid_ref, src_ref, dst_ref, h_hbm, wg_ref, wu_ref, wd_ref, wrow_ref,
                y_hbm, xbuf_ref, hb_ref, act_ref, ystage_ref, gsem, ssem):
    t, f = pl.program_id(0), pl.program_id(1)
    rows = hb_ref.shape[0]
    n_up = act_ref.shape[0]
    n_down = pl.num_programs(1) - n_up
    nv = nvalid_ref[0]
    valid = t < nv
    slot = lax.rem(t, 2)

    def gather_row(tile, buf, r, src_row):
        return pltpu.make_async_copy(h_hbm.at[pl.ds(src_row, 1), :],
                                     xbuf_ref.at[buf, pl.ds(r, 1), :], gsem.at[buf])

    def start_gather(tile, buf):
        def body(r, carry):
            gather_row(tile, buf, r, src_ref[tile * rows + r]).start()
            return carry
        lax.fori_loop(0, rows, body, 0, unroll=DMA_UNROLL)

    def wait_gather(tile, buf):
        def body(r, carry):
            gather_row(tile, buf, r, 0).wait()
            return carry
        lax.fori_loop(0, rows, body, 0, unroll=DMA_UNROLL)

    def scatter_row(r, dst_row):
        return pltpu.make_async_copy(ystage_ref.at[pl.ds(r, 1), :],
                                     y_hbm.at[pl.ds(dst_row, 1), :], ssem)

    def scatter(tile, wait):
        def body(r, carry):
            dst_row = dst_ref[tile * rows + r]

            @pl.when(dst_row >= 0)
            def _():
                if wait:
                    scatter_row(r, dst_row).wait()
                else:
                    scatter_row(r, dst_row).start()
            return carry
        lax.fori_loop(0, rows, body, 0, unroll=DMA_UNROLL)

    @pl.when(jnp.logical_and(valid, f == 0))
    def _():
        @pl.when(t == 0)
        def _():
            start_gather(t, slot)
        wait_gather(t, slot)
        hb_ref[...] = xbuf_ref[slot].astype(BF16)

        @pl.when(t + 1 < nv)
        def _():
            start_gather(t + 1, 1 - slot)

    @pl.when(jnp.logical_and(valid, f < n_up))
    def _():
        wgu = jnp.concatenate([wg_ref[0].astype(BF16), wu_ref[0].astype(BF16)], axis=1)
        gu = jnp.dot(hb_ref[...], wgu, preferred_element_type=F32)
        g, u = gu[:, :GMM_UP_COLS], gu[:, GMM_UP_COLS:]
        act_ref[f] = ((g * jax.nn.sigmoid(g)) * u).astype(BF16)

    @pl.when(jnp.logical_and(valid, jnp.logical_and(f == n_up, t > 0)))
    def _():
        scatter(t - 1, wait=True)

    for c in range(n_down):
        @pl.when(jnp.logical_and(valid, f == n_up + c))
        def _():
            act = jnp.concatenate([act_ref[k] for k in range(n_up)], axis=1)
            y = jnp.dot(act, wd_ref[0].astype(BF16), preferred_element_type=F32)
            ystage_ref[:, c * GMM_DOWN_COLS:(c + 1) * GMM_DOWN_COLS] = y * wrow_ref[...]

    @pl.when(jnp.logical_and(valid, f == n_up + n_down - 1))
    def _():
        scatter(t, wait=False)

        @pl.when(t == nv - 1)
        def _():
            scatter(t, wait=True)


def _gmm(eid, nvalid, src, dst, h, w_gate, w_up, w_down, wrow, n_slots):
    cap = src.shape[0]
    d = h.shape[1]
    d_exp = w_gate.shape[2]
    n_up = d_exp // GMM_UP_COLS
    n_down = d // GMM_DOWN_COLS
    assert d_exp % GMM_UP_COLS == 0 and d % GMM_DOWN_COLS == 0 and cap % GMM_ROWS == 0

    def up_block(t, f, e, nv, s, ds):
        return (e[t], 0, jnp.where(t < nv[0], jnp.minimum(f, n_up - 1), n_up - 1))

    def down_block(t, f, e, nv, s, ds):
        return (e[t], 0, jnp.where(t < nv[0], jnp.maximum(f - n_up, 0), n_down - 1))

    return pl.pallas_call(
        _gmm_kernel,
        out_shape=jax.ShapeDtypeStruct((n_slots, d), F32),
        grid_spec=pltpu.PrefetchScalarGridSpec(
            num_scalar_prefetch=4,
            grid=(cap // GMM_ROWS, n_up + n_down),
            in_specs=[
                pl.BlockSpec(memory_space=pl.ANY),
                pl.BlockSpec((1, d, GMM_UP_COLS), up_block),
                pl.BlockSpec((1, d, GMM_UP_COLS), up_block),
                pl.BlockSpec((1, d_exp, GMM_DOWN_COLS), down_block),
                pl.BlockSpec((GMM_ROWS, 1), lambda t, f, e, nv, s, ds: (t, 0)),
            ],
            out_specs=pl.BlockSpec(memory_space=pl.ANY),
            scratch_shapes=[pltpu.VMEM((2, GMM_ROWS, d), F32),
                            pltpu.VMEM((GMM_ROWS, d), BF16),
                            pltpu.VMEM((n_up, GMM_ROWS, GMM_UP_COLS), BF16),
                            pltpu.VMEM((GMM_ROWS, d), F32),
                            pltpu.SemaphoreType.DMA((2,)),
                            pltpu.SemaphoreType.DMA],
        ),
        compiler_params=_cparams(("arbitrary", "arbitrary"), 48),
        name="moe_gmm",
    )(eid, nvalid, src, dst, h, w_gate, w_up, w_down, wrow)


def _combine_kernel(x_ref, y0_ref, y1_ref, mod_ref, *rest, final):
    if final:
        gain_ref, o_ref = rest
    else:
        (o_ref,) = rest
    rows = min(NORM_ROWS, x_ref.shape[0])
    gate = mod_ref[0, 5:6, :]

    def body(c, carry):
        r0 = pl.multiple_of(c * rows, rows)
        y = y0_ref[pl.ds(r0, rows), :] + y1_ref[pl.ds(r0, rows), :]
        out = x_ref[pl.ds(r0, rows), :] + gate * y
        if final:
            ms = jnp.mean(out * out, axis=-1, keepdims=True)
            out = (out * lax.rsqrt(ms + NORM_EPS)) * gain_ref[...]
        o_ref[pl.ds(r0, rows), :] = out
        return carry

    lax.fori_loop(0, x_ref.shape[0] // rows, body, 0)


def _combine(x, y_slots, mod, seq, final_gain=None):
    n, d = x.shape
    tm = _pick(seq, 512)
    tiles_per_seq = seq // tm
    tiles = n // tm
    final = final_gain is not None
    in_specs = [pl.BlockSpec((tm, d), lambda i: (i, 0)),
                pl.BlockSpec((tm, d), lambda i: (i, 0)),
                pl.BlockSpec((tm, d), lambda i: (tiles + i, 0)),
                pl.BlockSpec((1, 6, d), lambda i: (i // tiles_per_seq, 0, 0))]
    args = [x, y_slots, y_slots, mod]
    if final:
        in_specs.append(pl.BlockSpec((1, d), lambda i: (0, 0)))
        args.append(final_gain.reshape(1, d))
    return pl.pallas_call(
        functools.partial(_combine_kernel, final=final),
        out_shape=jax.ShapeDtypeStruct((n, d), F32),
        grid=(tiles,),
        in_specs=in_specs,
        out_specs=pl.BlockSpec((tm, d), lambda i: (i, 0)),
        compiler_params=_cparams(("parallel",), 48),
        name="moe_combine",
    )(*args)


def _dispatch_plan(idx, wgt):
    n = idx.shape[1]
    slots = TOP_K * n
    cap = slots + N_EXPERTS * GMM_ROWS
    expert = idx.reshape(slots)
    onehot = (expert[:, None] == jnp.arange(N_EXPERTS)[None, :]).astype(jnp.int32)
    csum = jnp.cumsum(onehot, axis=0)
    rank = jnp.sum(csum * onehot, axis=1) - 1
    counts = csum[-1]
    padded = -(-counts // GMM_ROWS) * GMM_ROWS
    ends = jnp.cumsum(padded)
    pos = (ends - padded)[expert] + rank
    slot_ids = jnp.arange(slots, dtype=jnp.int32)
    src = jnp.zeros((cap,), jnp.int32).at[pos].set(slot_ids % n)
    dst = jnp.full((cap,), -1, jnp.int32).at[pos].set(slot_ids)
    wrow = jnp.zeros((cap,), F32).at[pos].set(wgt.reshape(slots))
    nvalid = (ends[-1] // GMM_ROWS).astype(jnp.int32)
    tile_start = jnp.arange(cap // GMM_ROWS, dtype=jnp.int32) * GMM_ROWS
    tile_start = jnp.minimum(tile_start, ends[-1] - GMM_ROWS)
    eid = jnp.sum(tile_start[:, None] >= ends[None, :], axis=1).astype(jnp.int32)
    return src, dst, wrow.reshape(cap, 1), eid, nvalid.reshape(1)


def _moe_layer(x, gain, mod, router_w, router_bias, w_gate, w_up, w_down, seq, final_gain):
    h, idx, wgt = _route(x, gain, mod, router_w, router_bias, seq)
    src, dst, wrow, eid, nvalid = _dispatch_plan(idx, wgt)
    y_slots = _gmm(eid, nvalid, src, dst, h, w_gate, w_up, w_down, wrow, TOP_K * x.shape[0])
    return _combine(x, y_slots, mod, seq, final_gain)


def _moba_fox_mixer(x, gain, mod, w_in, forget_bias, w_out, batch, seq):
    n, d = x.shape
    mix = (MOBA_HEADS + FOX_HEADS) * HEAD_DIM
    qk_scale = HEAD_DIM ** -0.5 * LOG2E
    s1 = MOBA_HEADS * HEAD_DIM
    s2 = FOX_HEADS * HEAD_DIM
    col_scale = jnp.ones((3 * mix,), F32)
    col_scale = col_scale.at[:s1].set(qk_scale).at[3 * s1:3 * s1 + s2].set(qk_scale)
    w_main = (w_in[:, :3 * mix] * col_scale[None, :]).astype(BF16)
    w_forget = jnp.pad(w_in[:, 3 * mix:], ((0, 0), (0, LANES - FOX_HEADS))).astype(BF16)
    tn = 512
    proj, flog = _matmul(
        x, w_main, seq=seq, prologue="norm", gain=gain, mod=mod, shift_row=0, scale_row=1,
        epilogue="rot", rot_tables=_rope_tables(seq, ROT_DIM), rot_groups=range(tn // LANES),
        rot_tile_limit=2 * s1 // tn, rot_half=ROT_DIM // 2, w_side=w_forget, tn=tn,
        name="l0_in_proj")
    qx, kx = _fox_prep(flog, forget_bias, batch, seq)
    o_moba = _attention(proj, proj, proj, batch=batch, seq=seq, heads=MOBA_HEADS,
                        dk=HEAD_DIM, dv=HEAD_DIM, q_col=0, k_col=MOBA_HEADS,
                        v_col=2 * MOBA_HEADS, mode="moba", name="moba_attn")
    base = 3 * MOBA_HEADS
    o_fox = _attention(proj, proj, proj, batch=batch, seq=seq, heads=FOX_HEADS,
                       dk=HEAD_DIM, dv=HEAD_DIM, q_col=base, k_col=base + FOX_HEADS,
                       v_col=base + 2 * FOX_HEADS, mode="fox", qx=qx, kx=kx,
                       name="fox_attn")
    o = jnp.concatenate([o_moba, o_fox], axis=1)
    return _matmul(o, w_out.astype(BF16), seq=seq, out_dtype=F32, epilogue="residual",
                   mod=mod, x_res=x, gate_row=2, name="l0_out_proj")


def _mla_down_kernel(x_ref, gain_ref, mod_ref, wd_ref, qn_ref, kvn_ref, cos_ref, up_ref, dn_ref,
                     cq_ref, kin_ref, hb_ref):
    def store(r0, rows, h):
        hb_ref[pl.ds(r0, rows), :] = h.astype(BF16)
    _norm_mod_rows(x_ref, gain_ref, mod_ref, 0, 1, store)
    a = jnp.dot(hb_ref[...], wd_ref[...], preferred_element_type=F32)

    def rms(v, g):
        ms = jnp.mean(v * v, axis=-1, keepdims=True)
        return (v * lax.rsqrt(ms + NORM_EPS)) * g

    q_end = MLA_Q_LORA
    kv_end = MLA_Q_LORA + MLA_KV_LORA
    cq_ref[...] = rms(a[:, :q_end], qn_ref[...]).astype(BF16)
    kin_ref[:, :MLA_KV_LORA] = rms(a[:, q_end:kv_end], kvn_ref[...]).astype(BF16)
    kpe = _rope_group(a[:, kv_end:kv_end + LANES], cos_ref[...], up_ref[...], dn_ref[...],
                      MLA_ROPE_DIM // 2)
    kin_ref[:, MLA_KV_LORA:] = kpe.astype(BF16)


def _mla_down(x, gain, mod, w_down, q_norm, kv_norm, tables, seq):
    n, d = x.shape
    tm = _pick(seq, 512)
    tiles_per_seq = seq // tm
    wcols = w_down.shape[1]
    const = lambda i: (0, 0)
    return pl.pallas_call(
        _mla_down_kernel,
        out_shape=[jax.ShapeDtypeStruct((n, MLA_Q_LORA), BF16),
                   jax.ShapeDtypeStruct((n, MLA_KV_LORA + LANES), BF16)],
        grid=(n // tm,),
        in_specs=[pl.BlockSpec((tm, d), lambda i: (i, 0)),
                  pl.BlockSpec((1, d), const),
                  pl.BlockSpec((1, 6, d), lambda i: (i // tiles_per_seq, 0, 0)),
                  pl.BlockSpec((d, wcols), const),
                  pl.BlockSpec((1, MLA_Q_LORA), const),
                  pl.BlockSpec((1, MLA_KV_LORA), const)]
                 + [pl.BlockSpec((tm, LANES), lambda i: (i % tiles_per_seq, 0))] * 3,
        out_specs=[pl.BlockSpec((tm, MLA_Q_LORA), lambda i: (i, 0)),
                   pl.BlockSpec((tm, MLA_KV_LORA + LANES), lambda i: (i, 0))],
        scratch_shapes=[pltpu.VMEM((tm, d), BF16)],
        compiler_params=_cparams(("parallel",), 40),
        name="mla_down",
    )(x, gain.reshape(1, d), mod, w_down, q_norm.reshape(1, -1), kv_norm.reshape(1, -1), *tables)


def _mla_mixer(x, gain, mod, w_dq, q_norm, w_uq, w_dkv, kv_norm, w_ukv, w_out, batch, seq):
    n, d = x.shape
    dk = 2 * LANES
    pad_rope = LANES - MLA_ROPE_DIM
    qk_scale = (MLA_NOPE_DIM + MLA_ROPE_DIM) ** -0.5 * LOG2E
    tables = _rope_tables(seq, MLA_ROPE_DIM)
    w_down = jnp.concatenate(
        [w_dq, w_dkv, jnp.zeros((d, pad_rope), F32)], axis=1).astype(BF16)
    cq, kin = _mla_down(x, gain, mod, w_down, q_norm, kv_norm, tables, seq)

    w_q = (w_uq * qk_scale).reshape(MLA_Q_LORA, MLA_HEADS, MLA_NOPE_DIM + MLA_ROPE_DIM)
    w_q = jnp.pad(w_q, ((0, 0), (0, 0), (0, pad_rope))).reshape(MLA_Q_LORA, MLA_HEADS * dk)
    tn = 512
    q = _matmul(cq, w_q.astype(BF16), seq=seq, epilogue="rot", rot_tables=tables,
                rot_groups=range(1, tn // LANES, 2), rot_half=MLA_ROPE_DIM // 2, tn=tn,
                name="mla_q_proj")

    w_kv = w_ukv.reshape(MLA_KV_LORA, MLA_HEADS, MLA_NOPE_DIM + MLA_V_DIM)
    w_knope = jnp.pad(w_kv[:, :, :MLA_NOPE_DIM], ((0, LANES), (0, 0), (0, LANES)))
    rope_place = jnp.eye(LANES, dk, k=LANES, dtype=F32) * (
        jnp.arange(LANES) < MLA_ROPE_DIM).astype(F32)[:, None]
    rope_rows = jnp.concatenate([jnp.zeros((MLA_KV_LORA, dk), F32), rope_place], axis=0)
    w_k = w_knope + rope_rows[:, None, :]
    w_v = jnp.pad(w_kv[:, :, MLA_NOPE_DIM:], ((0, LANES), (0, 0), (0, 0)))
    w_kv_all = jnp.concatenate([w_k.reshape(MLA_KV_LORA + LANES, MLA_HEADS * dk),
                                w_v.reshape(MLA_KV_LORA + LANES, MLA_HEADS * MLA_V_DIM)], axis=1)
    kv = _matmul(kin, w_kv_all.astype(BF16), seq=seq, tn=tn, name="mla_kv_proj")

    o = _attention(q, kv, kv, batch=batch, seq=seq, heads=MLA_HEADS, dk=dk, dv=MLA_V_DIM,
                   q_col=0, k_col=0, v_col=MLA_HEADS * dk // MLA_V_DIM, mode="plain",
                   name="mla_attn")
    return _matmul(o, w_out.astype(BF16), seq=seq, out_dtype=F32, epilogue="residual",
                   mod=mod, x_res=x, gate_row=2, name="l1_out_proj")


def kernel(x, c, ada_w, ada_b, norm_mix, norm_ffn, ab_w_in, ab_forget_bias, ab_w_out, mla_w_dq,
           mla_q_norm, mla_w_uq, mla_w_dkv, mla_kv_norm, mla_w_ukv, mla_w_out, router_w,
           router_bias, exp_w_gate, exp_w_up, exp_w_down, final_norm):
    batch, seq, d = x.shape
    depth = ada_w.shape[0]
    mod = _adaln(c, ada_w, ada_b)
    xs = x.reshape(batch * seq, d)
    for layer in range(depth):
        if layer % 2 == 0:
            xs = _moba_fox_mixer(xs, norm_mix[layer], mod[layer], ab_w_in, ab_forget_bias,
                                 ab_w_out, batch, seq)
        else:
            xs = _mla_mixer(xs, norm_mix[layer], mod[layer], mla_w_dq, mla_q_norm, mla_w_uq,
                            mla_w_dkv, mla_kv_norm, mla_w_ukv, mla_w_out, batch, seq)
        final_gain = final_norm if layer == depth - 1 else None
        xs = _moe_layer(xs, norm_ffn[layer], mod[layer], router_w, router_bias,
                        exp_w_gate[layer], exp_w_up[layer], exp_w_down[layer], seq, final_gain)
    return xs.reshape(batch, seq, d)
```

```python
import functools
import math

import jax
import jax.numpy as jnp
from jax import lax
from jax.experimental import pallas as pl
from jax.experimental.pallas import tpu as pltpu

F32 = jnp.float32
BF16 = jnp.bfloat16

HEAD_DIM = 128
MOBA_HEADS = 8
FOX_HEADS = 8
ROT_DIM = HEAD_DIM // 4
ROPE_THETA = 500000.0
MOBA_BLOCK = 256
MOBA_TOPK = 3
MLA_HEADS = 16
MLA_Q_LORA = 512
MLA_KV_LORA = 512
MLA_NOPE_DIM = 128
MLA_ROPE_DIM = 64
MLA_V_DIM = 128
N_EXPERTS = 16
N_GROUPS = 4
EXPERTS_PER_GROUP = N_EXPERTS // N_GROUPS
TOP_K = 2
NORM_EPS = 1e-6

LANES = 128
V7X_VMEM_BYTES = 64 * 1024 * 1024
MIB = 1024 * 1024
LOG2E = math.log2(math.e)

NT_DIMS = (((1,), (1,)), ((), ()))


def _cparams(semantics, vmem_mib):
    assert vmem_mib * MIB < V7X_VMEM_BYTES
    return pltpu.CompilerParams(dimension_semantics=semantics,
                                vmem_limit_bytes=vmem_mib * MIB)


def _pick(n, pref):
    t = min(pref, n)
    while n % t:
        t //= 2
    return t


def _adaln_kernel(c_ref, w_ref, b_ref, o_ref):
    c = c_ref[...]
    ca = (c * jax.nn.sigmoid(c)).astype(BF16)
    acc = jnp.dot(ca, w_ref[0].astype(BF16), preferred_element_type=F32)
    o_ref[0] = acc + b_ref[0]


def _adaln(c, ada_w, ada_b):
    depth, d, n6 = ada_w.shape
    b = c.shape[0]
    rows = 8
    c_pad = jnp.pad(c, ((0, rows - b), (0, 0)))
    tn = _pick(n6, 1024)
    out = pl.pallas_call(
        _adaln_kernel,
        out_shape=jax.ShapeDtypeStruct((depth, rows, n6), F32),
        grid=(depth, n6 // tn),
        in_specs=[
            pl.BlockSpec((rows, d), lambda l, j: (0, 0)),
            pl.BlockSpec((1, d, tn), lambda l, j: (l, 0, j)),
            pl.BlockSpec((1, 1, tn), lambda l, j: (l, 0, j)),
        ],
        out_specs=pl.BlockSpec((1, rows, tn), lambda l, j: (l, 0, j)),
        compiler_params=_cparams(("parallel", "parallel"), 40),
        name="adaln_mod",
    )(c_pad, ada_w, ada_b.reshape(depth, 1, n6))
    return out[:, :b, :].reshape(depth, b, 6, d)


NORM_ROWS = 256


def _norm_mod_rows(x_ref, gain_ref, mod_ref, shift_row, scale_row, store):
    tm = x_ref.shape[0]
    rows = min(NORM_ROWS, tm)
    gain = gain_ref[...]
    scale1p = 1.0 + mod_ref[0, scale_row:scale_row + 1, :]
    shift = mod_ref[0, shift_row:shift_row + 1, :]

    def body(c, carry):
        r0 = pl.multiple_of(c * rows, rows)
        x = x_ref[pl.ds(r0, rows), :]
        ms = jnp.mean(x * x, axis=-1, keepdims=True)
        y = x * lax.rsqrt(ms + NORM_EPS)
        h = (y * gain) * scale1p + shift
        store(r0, rows, h)
        return carry

    lax.fori_loop(0, tm // rows, body, 0)


def _rope_group(a, cos_t, sin_up, sin_dn, half):
    up = pltpu.roll(a, LANES - half, 1)
    dn = pltpu.roll(a, half, 1)
    return a * cos_t + up * sin_up + dn * sin_dn


def _rope_tables(seq, rot_dim):
    half = rot_dim // 2
    inv_freq = ROPE_THETA ** (-jnp.arange(half, dtype=F32) / half)
    ang = jnp.arange(seq, dtype=F32)[:, None] * inv_freq[None, :]
    cos, sin = jnp.cos(ang), jnp.sin(ang)
    zeros = jnp.zeros((seq, LANES - rot_dim), F32)
    zh = jnp.zeros((seq, half), F32)
    cos_t = jnp.concatenate([cos, cos, jnp.ones((seq, LANES - rot_dim), F32)], axis=1)
    sin_up = jnp.concatenate([-sin, zh, zeros], axis=1)
    sin_dn = jnp.concatenate([zh, sin, zeros], axis=1)
    return cos_t, sin_up, sin_dn


def _mm_kernel(*refs, prologue, epilogue, rot_groups, rot_tile_limit, rot_half,
               shift_row, scale_row, gate_row, has_side):
    it = iter(refs)
    a_ref = next(it)
    if prologue == "norm":
        gain_ref = next(it)
    if prologue == "norm" or epilogue == "residual":
        mod_ref = next(it)
    w_ref = next(it)
    if has_side:
        wside_ref = next(it)
    if epilogue == "rot":
        cos_ref, up_ref, dn_ref = next(it), next(it), next(it)
    if epilogue == "residual":
        x_ref = next(it)
    o_ref = next(it)
    if has_side:
        side_ref = next(it)
    if prologue == "norm":
        hb_ref = next(it)

    j = pl.program_id(1)

    if prologue == "norm":
        @pl.when(j == 0)
        def _():
            def store(r0, rows, h):
                hb_ref[pl.ds(r0, rows), :] = h.astype(BF16)
            _norm_mod_rows(a_ref, gain_ref, mod_ref, shift_row, scale_row, store)
            if has_side:
                side_ref[...] = jnp.dot(hb_ref[...], wside_ref[...],
                                        preferred_element_type=F32)
        lhs = hb_ref[...]
    else:
        lhs = a_ref[...]

    acc = jnp.dot(lhs, w_ref[...], preferred_element_type=F32)

    if epilogue == "rot":
        def rotated():
            groups = []
            for g in range(acc.shape[1] // LANES):
                blk = acc[:, g * LANES:(g + 1) * LANES]
                if g in rot_groups:
                    blk = _rope_group(blk, cos_ref[...], up_ref[...], dn_ref[...], rot_half)
                groups.append(blk)
            return jnp.concatenate(groups, axis=1)

        if rot_tile_limit is None:
            o_ref[...] = rotated().astype(o_ref.dtype)
        else:
            @pl.when(j < rot_tile_limit)
            def _():
                o_ref[...] = rotated().astype(o_ref.dtype)

            @pl.when(j >= rot_tile_limit)
            def _():
                o_ref[...] = acc.astype(o_ref.dtype)
    elif epilogue == "residual":
        gate = mod_ref[0, gate_row:gate_row + 1, :]
        o_ref[...] = x_ref[...] + gate * acc
    else:
        o_ref[...] = acc.astype(o_ref.dtype)


def _matmul(a, w, *, seq, out_dtype=BF16, tm=1024, tn=512, prologue="none", gain=None,
            mod=None, shift_row=0, scale_row=1, epilogue="none", rot_tables=None,
            rot_groups=(), rot_tile_limit=None, rot_half=0, x_res=None, gate_row=2,
            w_side=None, name="matmul"):
    m, k = a.shape
    n = w.shape[1]
    tm = _pick(seq, tm)
    tn = _pick(n, tn)
    tiles_per_seq = seq // tm
    has_side = w_side is not None
    in_specs = [pl.BlockSpec((tm, k), lambda i, j: (i, 0))]
    args = [a]
    if prologue == "norm":
        in_specs.append(pl.BlockSpec((1, k), lambda i, j: (0, 0)))
        args.append(gain.reshape(1, k))
    if prologue == "norm":
        in_specs.append(pl.BlockSpec((1, 6, k), lambda i, j: (i // tiles_per_seq, 0, 0)))
        args.append(mod)
    elif epilogue == "residual":
        in_specs.append(pl.BlockSpec((1, 6, tn), lambda i, j: (i // tiles_per_seq, 0, j)))
        args.append(mod)
    in_specs.append(pl.BlockSpec((k, tn), lambda i, j: (0, j)))
    args.append(w)
    if has_side:
        in_specs.append(pl.BlockSpec((k, LANES), lambda i, j: (0, 0)))
        args.append(w_side)
    if epilogue == "rot":
        for t in rot_tables:
            in_specs.append(pl.BlockSpec((tm, LANES), lambda i, j: (i % tiles_per_seq, 0)))
            args.append(t)
    if epilogue == "residual":
        in_specs.append(pl.BlockSpec((tm, tn), lambda i, j: (i, j)))
        args.append(x_res)
    out_shape = [jax.ShapeDtypeStruct((m, n), out_dtype)]
    out_specs = [pl.BlockSpec((tm, tn), lambda i, j: (i, j))]
    if has_side:
        out_shape.append(jax.ShapeDtypeStruct((m, LANES), F32))
        out_specs.append(pl.BlockSpec((tm, LANES), lambda i, j: (i, 0)))
    scratch = [pltpu.VMEM((tm, k), BF16)] if prologue == "norm" else []
    kernel = functools.partial(
        _mm_kernel, prologue=prologue, epilogue=epilogue, rot_groups=tuple(rot_groups),
        rot_tile_limit=rot_tile_limit, rot_half=rot_half, shift_row=shift_row,
        scale_row=scale_row, gate_row=gate_row, has_side=has_side)
    outs = pl.pallas_call(
        kernel,
        out_shape=out_shape,
        grid=(m // tm, n // tn),
        in_specs=in_specs,
        out_specs=out_specs,
        scratch_shapes=scratch,
        compiler_params=_cparams(("parallel", "arbitrary"), 48),
        name=name,
    )(*args)
    return outs if has_side else outs[0]


def _fox_prep_kernel(flog_ref, bias_ref, qx_ref, kx_ref, carry_ref):
    tc = flog_ref.shape[0]

    @pl.when(pl.program_id(1) == 0)
    def _():
        carry_ref[...] = jnp.zeros_like(carry_ref)

    z = flog_ref[...] + bias_ref[...]
    logf = jnp.minimum(z, 0.0) - jnp.log1p(jnp.exp(-jnp.abs(z)))
    row = lax.broadcasted_iota(jnp.int32, (tc, tc), 0)
    col = lax.broadcasted_iota(jnp.int32, (tc, tc), 1)
    tri = jnp.where(col <= row, 1.0, 0.0).astype(F32)
    csum = jnp.dot(tri, logf, preferred_element_type=F32,
                   precision=lax.Precision.HIGHEST) + carry_ref[...]
    carry_ref[...] = csum[tc - 1:tc, :]
    csum = csum * LOG2E
    lane = lax.broadcasted_iota(jnp.int32, (tc, LANES), 1)
    one = jnp.ones((tc, LANES), F32)
    zero = jnp.zeros((tc, LANES), F32)
    for h in range(FOX_HEADS):
        colv = jnp.broadcast_to(csum[:, h:h + 1], (tc, LANES))
        hi = colv.astype(BF16).astype(F32)
        r1 = colv - hi
        mid = r1.astype(BF16).astype(F32)
        lo = r1 - mid
        qv = jnp.where(lane == 0, hi, jnp.where(lane == 1, mid, jnp.where(
            lane == 2, lo, jnp.where(lane < 6, one, zero))))
        kv = jnp.where(lane < 3, one, jnp.where(lane == 3, -hi, jnp.where(
            lane == 4, -mid, jnp.where(lane == 5, -lo, zero))))
        qx_ref[:, h * LANES:(h + 1) * LANES] = qv.astype(BF16)
        kx_ref[:, h * LANES:(h + 1) * LANES] = kv.astype(BF16)


def _fox_prep(flog, forget_bias, batch, seq):
    n = flog.shape[0]
    tc = _pick(seq, 256)
    bias = jnp.pad(forget_bias, (0, LANES - FOX_HEADS)).reshape(1, LANES)
    width = FOX_HEADS * LANES
    steps = seq // tc
    return pl.pallas_call(
        _fox_prep_kernel,
        out_shape=[jax.ShapeDtypeStruct((n, width), BF16)] * 2,
        grid=(batch, steps),
        in_specs=[pl.BlockSpec((tc, LANES), lambda b, s: (b * steps + s, 0)),
                  pl.BlockSpec((1, LANES), lambda b, s: (0, 0))],
        out_specs=[pl.BlockSpec((tc, width), lambda b, s: (b * steps + s, 0))] * 2,
        scratch_shapes=[pltpu.VMEM((1, LANES), F32)],
        compiler_params=_cparams(("parallel", "arbitrary"), 32),
        name="fox_prep",
    )(flog, bias)


ATTN_TILE = 512
ATTN_HEADS_PER_STEP = 2


def _attn_kernel(*refs, mode, tq, dk, dv):
    it = iter(refs)
    q_ref, k_ref, v_ref = next(it), next(it), next(it)
    if mode == "fox":
        qx_ref, kx_ref = next(it), next(it)
    o_ref = next(it)
    m_ref, l_ref, acc_ref = next(it), next(it), next(it)
    if mode == "moba":
        kmean_ref, sel_ref = next(it), next(it)

    i = pl.program_id(2)
    neg_inf = jnp.float32(-jnp.inf)
    heads = q_ref.shape[1] // dk
    blocks_per_tile = tq // MOBA_BLOCK
    block_shift = MOBA_BLOCK.bit_length() - 1

    def q_of(h):
        q = q_ref[:, h * dk:(h + 1) * dk]
        if mode == "fox":
            q = jnp.concatenate([q, qx_ref[:, h * LANES:(h + 1) * LANES]], axis=1)
        return q

    def scores(h, q, j):
        r0 = pl.multiple_of(j * tq, tq)
        k = k_ref[pl.ds(r0, tq), h * dk:(h + 1) * dk]
        if mode == "fox":
            k = jnp.concatenate([k, kx_ref[pl.ds(r0, tq), h * LANES:(h + 1) * LANES]], axis=1)
        s = lax.dot_general(q, k, NT_DIMS, preferred_element_type=F32)
        return s, v_ref[pl.ds(r0, tq), h * dv:(h + 1) * dv]

    def block_hits(h, j):
        lane = lax.broadcasted_iota(jnp.int32, (tq, LANES), 1)
        sel = sel_ref[h]
        return [jnp.max(jnp.where(lane == j * blocks_per_tile + c, sel, 0.0), axis=1,
                        keepdims=True) for c in range(blocks_per_tile)]

    if mode == "moba":
        nb = k_ref.shape[0] // MOBA_BLOCK

        @pl.when(i == 0)
        def _():
            kmean_ref[...] = jnp.zeros_like(kmean_ref)
            for h in range(heads):
                for b in range(nb):
                    kb = k_ref[b * MOBA_BLOCK:(b + 1) * MOBA_BLOCK, h * dk:(h + 1) * dk]
                    kmean_ref[h, b:b + 1, :] = jnp.mean(kb.astype(F32), axis=0, keepdims=True)

        lane = lax.broadcasted_iota(jnp.int32, (tq, LANES), 1)
        lane_f = lane.astype(F32)
        row_block = jnp.right_shift(lax.broadcasted_iota(jnp.int32, (tq, LANES), 0), block_shift)
        past = lane < i * blocks_per_tile + row_block
        for h in range(heads):
            gate = lax.dot_general(q_of(h).astype(F32), kmean_ref[h], NT_DIMS,
                                   preferred_element_type=F32,
                                   precision=lax.Precision.HIGHEST)
            g = jnp.where(past, gate, neg_inf)
            chosen = jnp.zeros(gate.shape, F32)
            for _ in range(min(MOBA_TOPK, nb - 1)):
                mx = jnp.max(g, axis=1, keepdims=True)
                first = jnp.min(jnp.where(g == mx, lane_f, float(LANES)), axis=1,
                                keepdims=True)
                pick = lane_f == first
                chosen = jnp.where(jnp.logical_and(pick, past), 1.0, chosen)
                g = jnp.where(pick, neg_inf, g)
            sel_ref[h] = chosen

    row = lax.broadcasted_iota(jnp.int32, (tq, tq), 0)
    col = lax.broadcasted_iota(jnp.int32, (tq, tq), 1)
    row_blk = jnp.right_shift(row, block_shift)
    col_blk = jnp.right_shift(col, block_shift)
    for h in range(heads):
        s, v = scores(h, q_of(h), i)
        if mode == "moba":
            own = jnp.logical_and(row_blk == col_blk, col <= row)
            weight = jnp.where(own, 1.0, 0.0)
            hits = block_hits(h, i)
            for c in range(blocks_per_tile - 1):
                earlier = jnp.logical_and(col_blk == c, row_blk > c)
                weight = jnp.where(earlier, hits[c], weight)
            allowed = weight > 0.0
        else:
            allowed = col <= row
        s = jnp.where(allowed, s, neg_inf)
        m0 = jnp.max(s, axis=1, keepdims=True)
        p = jnp.exp2(s - m0)
        m_ref[h] = m0
        l_ref[h] = jnp.sum(p, axis=1, keepdims=True)
        acc_ref[h] = jnp.dot(p.astype(BF16), v, preferred_element_type=F32)

    def body(j, carry):
        for h in range(heads):
            s, v = scores(h, q_of(h), j)
            if mode == "moba":
                hits = block_hits(h, j)
                weight = hits[blocks_per_tile - 1]
                for c in range(blocks_per_tile - 2, -1, -1):
                    weight = jnp.where(col_blk == c, hits[c], weight)
                s = jnp.where(weight > 0.0, s, neg_inf)
            m_old = m_ref[h]
            m_new = jnp.maximum(m_old, jnp.max(s, axis=1, keepdims=True))
            alpha = jnp.exp2(m_old - m_new)
            p = jnp.exp2(s - m_new)
            l_ref[h] = alpha * l_ref[h] + jnp.sum(p, axis=1, keepdims=True)
            acc_ref[h] = alpha * acc_ref[h] + jnp.dot(p.astype(BF16), v,
                                                      preferred_element_type=F32)
            m_ref[h] = m_new
        return carry

    lax.fori_loop(0, i, body, 0)
    for h in range(heads):
        o_ref[:, h * dv:(h + 1) * dv] = (acc_ref[h] / l_ref[h]).astype(o_ref.dtype)


def _attention(q_arr, k_arr, v_arr, *, batch, seq, heads, dk, dv, q_col, k_col, v_col, mode,
               qx=None, kx=None, name="attn"):
    n = batch * seq
    tq = _pick(seq, ATTN_TILE)
    g = ATTN_HEADS_PER_STEP
    assert tq % MOBA_BLOCK == 0 and heads % g == 0
    assert q_col % g == 0 and k_col % g == 0 and v_col % g == 0
    nq = seq // tq
    in_specs = [
        pl.BlockSpec((tq, g * dk), lambda b, h, i: (b * nq + i, q_col // g + h)),
        pl.BlockSpec((seq, g * dk), lambda b, h, i: (b, k_col // g + h)),
        pl.BlockSpec((seq, g * dv), lambda b, h, i: (b, v_col // g + h)),
    ]
    args = [q_arr, k_arr, v_arr]
    if mode == "fox":
        in_specs += [pl.BlockSpec((tq, g * LANES), lambda b, h, i: (b * nq + i, h)),
                     pl.BlockSpec((seq, g * LANES), lambda b, h, i: (b, h))]
        args += [qx, kx]
    scratch = [pltpu.VMEM((g, tq, 1), F32), pltpu.VMEM((g, tq, 1), F32),
               pltpu.VMEM((g, tq, dv), F32)]
    if mode == "moba":
        scratch += [pltpu.VMEM((g, LANES, dk), F32), pltpu.VMEM((g, tq, LANES), F32)]
    return pl.pallas_call(
        functools.partial(_attn_kernel, mode=mode, tq=tq, dk=dk, dv=dv),
        out_shape=jax.ShapeDtypeStruct((n, heads * dv), BF16),
        grid=(batch, heads // g, nq),
        in_specs=in_specs,
        out_specs=pl.BlockSpec((tq, g * dv), lambda b, h, i: (b * nq + i, h)),
        scratch_shapes=scratch,
        compiler_params=_cparams(("parallel", "parallel", "arbitrary"), 40),
        name=name,
    )(*args)


def _top2_of4(vals):
    a, b, c, d = vals
    hi1, lo1 = jnp.maximum(a, b), jnp.minimum(a, b)
    hi2, lo2 = jnp.maximum(c, d), jnp.minimum(c, d)
    return jnp.maximum(hi1, hi2) + jnp.maximum(jnp.minimum(hi1, hi2), jnp.maximum(lo1, lo2))


def _route_kernel(x_ref, gain_ref, mod_ref, rwt_ref, rbias_ref, h_ref, idx_ref, wgt_ref):
    def store(r0, rows, h):
        h_ref[pl.ds(r0, rows), :] = h
    _norm_mod_rows(x_ref, gain_ref, mod_ref, 3, 4, store)

    logits = lax.dot_general(rwt_ref[...], h_ref[...], NT_DIMS, preferred_element_type=F32,
                             precision=lax.Precision.HIGHEST)
    score = jax.nn.sigmoid(logits)
    biased = score + rbias_ref[...]
    rows_b = [biased[e:e + 1, :] for e in range(N_EXPERTS)]
    rows_s = [score[e:e + 1, :] for e in range(N_EXPERTS)]
    grp = [_top2_of4(rows_b[g * EXPERTS_PER_GROUP:(g + 1) * EXPERTS_PER_GROUP])
           for g in range(N_GROUPS)]
    best, g_sel = grp[0], jnp.zeros_like(grp[0], dtype=jnp.int32)
    for g in range(1, N_GROUPS):
        better = grp[g] > best
        best = jnp.where(better, grp[g], best)
        g_sel = jnp.where(better, g, g_sel)

    def in_group(rows, r):
        out = rows[r]
        for g in range(1, N_GROUPS):
            out = jnp.where(g_sel == g, rows[g * EXPERTS_PER_GROUP + r], out)
        return out

    cb = [in_group(rows_b, r) for r in range(EXPERTS_PER_GROUP)]
    cs = [in_group(rows_s, r) for r in range(EXPERTS_PER_GROUP)]
    neg_inf = jnp.float32(-jnp.inf)
    picks = []
    for _ in range(TOP_K):
        bv, bi, bs = cb[0], jnp.zeros_like(g_sel), cs[0]
        for r in range(1, EXPERTS_PER_GROUP):
            better = cb[r] > bv
            bv = jnp.where(better, cb[r], bv)
            bi = jnp.where(better, r, bi)
            bs = jnp.where(better, cs[r], bs)
        picks.append((bi, bs))
        cb = [jnp.where(bi == r, neg_inf, cb[r]) for r in range(EXPERTS_PER_GROUP)]
    (i0, s0), (i1, s1) = picks
    total = s0 + s1
    idx_ref[0:1, :] = g_sel * EXPERTS_PER_GROUP + i0
    idx_ref[1:2, :] = g_sel * EXPERTS_PER_GROUP + i1
    wgt_ref[0:1, :] = s0 / total
    wgt_ref[1:2, :] = s1 / total


def _route(x, gain, mod, router_w, router_bias, seq):
    n, d = x.shape
    tm = _pick(seq, 512)
    tiles_per_seq = seq // tm
    return pl.pallas_call(
        _route_kernel,
        out_shape=[jax.ShapeDtypeStruct((n, d), F32),
                   jax.ShapeDtypeStruct((TOP_K, n), jnp.int32),
                   jax.ShapeDtypeStruct((TOP_K, n), F32)],
        grid=(n // tm,),
        in_specs=[pl.BlockSpec((tm, d), lambda i: (i, 0)),
                  pl.BlockSpec((1, d), lambda i: (0, 0)),
                  pl.BlockSpec((1, 6, d), lambda i: (i // tiles_per_seq, 0, 0)),
                  pl.BlockSpec((N_EXPERTS, d), lambda i: (0, 0)),
                  pl.BlockSpec((N_EXPERTS, 1), lambda i: (0, 0))],
        out_specs=[pl.BlockSpec((tm, d), lambda i: (i, 0)),
                   pl.BlockSpec((TOP_K, tm), lambda i: (0, i)),
                   pl.BlockSpec((TOP_K, tm), lambda i: (0, i))],
        compiler_params=_cparams(("parallel",), 32),
        name="moe_route",
    )(x, gain.reshape(1, d), mod, router_w.T, router_bias.reshape(N_EXPERTS, 1))


GMM_ROWS = 512
GMM_UP_COLS = 128
GMM_DOWN_COLS = 512
GMM_ISSUE_STEPS = 8
PACK_ROWS = 256
DMA_UNROLL = 8


def _pack_kernel(wg_ref, wu_ref, wd_ref, wgu_ref, wdb_ref):
    n_up = wgu_ref.shape[1]
    for f in range(n_up):
        cols = slice(f * GMM_UP_COLS, (f + 1) * GMM_UP_COLS)
        wgu_ref[0, f, :, :GMM_UP_COLS] = wg_ref[0, 0, :, cols].astype(BF16)
        wgu_ref[0, f, :, GMM_UP_COLS:] = wu_ref[0, 0, :, cols].astype(BF16)
    wdb_ref[0] = wd_ref[0, 0].astype(BF16)


def _pack_expert_weights(w_gate, w_up, w_down, layer):
    _, n_exp, d, d_exp = w_gate.shape
    n_up = d_exp // GMM_UP_COLS
    steps = d // PACK_ROWS
    down_rows = d_exp // steps
    assert d_exp % GMM_UP_COLS == 0 and d % PACK_ROWS == 0 and down_rows % 16 == 0
    up_spec = pl.BlockSpec((1, 1, PACK_ROWS, d_exp), lambda e, k: (layer, e, k, 0))
    return pl.pallas_call(
        _pack_kernel,
        out_shape=[jax.ShapeDtypeStruct((n_exp, n_up, d, 2 * GMM_UP_COLS), BF16),
                   jax.ShapeDtypeStruct((n_exp, d_exp, d), BF16)],
        grid=(n_exp, steps),
        in_specs=[up_spec, up_spec,
                  pl.BlockSpec((1, 1, down_rows, d), lambda e, k: (layer, e, k, 0))],
        out_specs=[pl.BlockSpec((1, n_up, PACK_ROWS, 2 * GMM_UP_COLS), lambda e, k: (e, 0, k, 0)),
                   pl.BlockSpec((1, down_rows, d), lambda e, k: (e, k, 0))],
        compiler_params=_cparams(("parallel", "parallel"), 32),
        name="moe_pack_weights",
    )(w_gate, w_up, w_down)


def _ffn_kernel(eid_ref, nvalid_ref, dst_ref, h_hbm, wgu_ref, wd_ref, wrow_ref, y_hbm,
                xbuf_ref, hb_ref, act_ref, ystage_ref, gsem, ssem):
    t, f = pl.program_id(0), pl.program_id(1)
    rows = hb_ref.shape[0]
    n_up = act_ref.shape[0]
    n_tokens = h_hbm.shape[0]
    per_step = rows // GMM_ISSUE_STEPS
    nv = nvalid_ref[0]
    valid = t < nv
    slot = lax.rem(t, 2)

    def dst_of(tile, r):
        return dst_ref[(tile + 1) * rows + r]

    def token_of(dst_row):
        if n_tokens & (n_tokens - 1) == 0:
            return jnp.bitwise_and(dst_row, n_tokens - 1)
        return lax.rem(dst_row, n_tokens)

    def gather_row(buf, r, src_row):
        return pltpu.make_async_copy(h_hbm.at[pl.ds(src_row, 1), :],
                                     xbuf_ref.at[buf, pl.ds(r, 1), :], gsem.at[buf])

    def scatter_row(r, dst_row):
        return pltpu.make_async_copy(ystage_ref.at[pl.ds(r, 1), :],
                                     y_hbm.at[pl.ds(dst_row, 1), :], ssem)

    def for_rows(fn):
        def body(r, carry):
            fn(r)
            return carry
        lax.fori_loop(0, rows, body, 0, unroll=DMA_UNROLL)

    @pl.when(jnp.logical_and(t == 0, f == 0))
    def _():
        ystage_ref[...] = jnp.zeros_like(ystage_ref)
        for_rows(lambda r: gather_row(0, r, token_of(dst_of(0, r))).start())

    @pl.when(jnp.logical_and(t <= nv, f == 0))
    def _():
        for_rows(lambda r: gather_row(slot, r, 0).wait())

    @pl.when(jnp.logical_and(t == nv, f == 0))
    def _():
        for_rows(lambda r: scatter_row(r, dst_of(t - 1, r)).start())

    @pl.when(jnp.logical_and(valid, f == 0))
    def _():
        hb_ref[...] = xbuf_ref[slot].astype(BF16)

    def up_step():
        gu = jnp.dot(hb_ref[...], wgu_ref[0, 0], preferred_element_type=F32)
        g, u = gu[:, :GMM_UP_COLS], gu[:, GMM_UP_COLS:]
        act_ref[f] = ((g * jax.nn.sigmoid(g)) * u).astype(BF16)

    @pl.when(jnp.logical_and(valid, f < GMM_ISSUE_STEPS))
    def _():
        for k in range(per_step):
            r = f * per_step + k
            gather_row(1 - slot, r, token_of(dst_of(t + 1, r))).start()
            scatter_row(r, dst_of(t - 1, r)).start()
        up_step()

    @pl.when(jnp.logical_and(valid, jnp.logical_and(f >= GMM_ISSUE_STEPS, f < n_up)))
    def _():
        up_step()

    @pl.when(jnp.logical_and(t <= nv, f == n_up))
    def _():
        for_rows(lambda r: scatter_row(r, 0).wait())

    @pl.when(jnp.logical_and(valid, f == n_up))
    def _():
        act = jnp.concatenate([act_ref[k] for k in range(n_up)], axis=1)
        y = jnp.dot(act, wd_ref[0], preferred_element_type=F32)
        ystage_ref[...] = y * wrow_ref[...]


def _expert_ffn(eid, nvalid, dst, h, wgu, wdb, wrow, n_slots):
    n_tiles = dst.shape[0] // GMM_ROWS - 1
    d = h.shape[1]
    n_exp, n_up, _, up_cols = wgu.shape
    d_exp = wdb.shape[1]
    assert GMM_ROWS % GMM_ISSUE_STEPS == 0 and n_up >= GMM_ISSUE_STEPS

    def up_block(t, f, e, nv, ds):
        return (e[t], jnp.where(t < nv[0], jnp.minimum(f, n_up - 1), n_up - 1), 0, 0)

    return pl.pallas_call(
        _ffn_kernel,
        out_shape=jax.ShapeDtypeStruct((n_slots + GMM_ROWS, d), F32),
        grid_spec=pltpu.PrefetchScalarGridSpec(
            num_scalar_prefetch=3,
            grid=(n_tiles, n_up + 1),
            in_specs=[
                pl.BlockSpec(memory_space=pl.ANY),
                pl.BlockSpec((1, 1, d, up_cols), up_block),
                pl.BlockSpec((1, d_exp, d), lambda t, f, e, nv, ds: (e[t], 0, 0)),
                pl.BlockSpec((GMM_ROWS, 1), lambda t, f, e, nv, ds: (t, 0)),
            ],
            out_specs=pl.BlockSpec(memory_space=pl.ANY),
            scratch_shapes=[pltpu.VMEM((2, GMM_ROWS, d), F32),
                            pltpu.VMEM((GMM_ROWS, d), BF16),
                            pltpu.VMEM((n_up, GMM_ROWS, GMM_UP_COLS), BF16),
                            pltpu.VMEM((GMM_ROWS, d), F32),
                            pltpu.SemaphoreType.DMA((2,)),
                            pltpu.SemaphoreType.DMA],
        ),
        compiler_params=_cparams(("arbitrary", "arbitrary"), 48),
        name="moe_ffn",
    )(eid, nvalid, dst, h, wgu, wdb, wrow)


def _combine_kernel(x_ref, y0_ref, y1_ref, mod_ref, *rest, final):
    if final:
        gain_ref, o_ref = rest
    else:
        (o_ref,) = rest
    rows = min(NORM_ROWS, x_ref.shape[0])
    gate = mod_ref[0, 5:6, :]

    def body(c, carry):
        r0 = pl.multiple_of(c * rows, rows)
        y = y0_ref[pl.ds(r0, rows), :] + y1_ref[pl.ds(r0, rows), :]
        out = x_ref[pl.ds(r0, rows), :] + gate * y
        if final:
            ms = jnp.mean(out * out, axis=-1, keepdims=True)
            out = (out * lax.rsqrt(ms + NORM_EPS)) * gain_ref[...]
        o_ref[pl.ds(r0, rows), :] = out
        return carry

    lax.fori_loop(0, x_ref.shape[0] // rows, body, 0)


def _combine(x, y_slots, mod, seq, final_gain=None):
    n, d = x.shape
    tm = _pick(seq, 512)
    tiles_per_seq = seq // tm
    tiles = n // tm
    final = final_gain is not None
    in_specs = [pl.BlockSpec((tm, d), lambda i: (i, 0)),
                pl.BlockSpec((tm, d), lambda i: (i, 0)),
                pl.BlockSpec((tm, d), lambda i: (tiles + i, 0)),
                pl.BlockSpec((1, 6, d), lambda i: (i // tiles_per_seq, 0, 0))]
    args = [x, y_slots, y_slots, mod]
    if final:
        in_specs.append(pl.BlockSpec((1, d), lambda i: (0, 0)))
        args.append(final_gain.reshape(1, d))
    return pl.pallas_call(
        functools.partial(_combine_kernel, final=final),
        out_shape=jax.ShapeDtypeStruct((n, d), F32),
        grid=(tiles,),
        in_specs=in_specs,
        out_specs=pl.BlockSpec((tm, d), lambda i: (i, 0)),
        compiler_params=_cparams(("parallel",), 48),
        name="moe_combine",
    )(*args)


def _dispatch_plan(idx, wgt):
    n = idx.shape[1]
    slots = TOP_K * n
    cap = slots + N_EXPERTS * GMM_ROWS
    expert = idx.reshape(slots)
    onehot = (expert[:, None] == jnp.arange(N_EXPERTS)[None, :]).astype(jnp.int32)
    csum = jnp.cumsum(onehot, axis=0)
    rank = jnp.sum(csum * onehot, axis=1) - 1
    counts = csum[-1]
    padded = -(-counts // GMM_ROWS) * GMM_ROWS
    ends = jnp.cumsum(padded)
    pos = (ends - padded)[expert] + rank
    slot_ids = jnp.arange(slots, dtype=jnp.int32)
    dummy = slots + jnp.arange(cap + GMM_ROWS, dtype=jnp.int32) % GMM_ROWS
    dst = dummy.at[pos + GMM_ROWS].set(slot_ids)
    wrow = jnp.zeros((cap,), F32).at[pos].set(wgt.reshape(slots))
    nvalid = (ends[-1] // GMM_ROWS).astype(jnp.int32)
    tile_start = jnp.arange(cap // GMM_ROWS, dtype=jnp.int32) * GMM_ROWS
    tile_start = jnp.minimum(tile_start, ends[-1] - GMM_ROWS)
    eid = jnp.sum(tile_start[:, None] >= ends[None, :], axis=1).astype(jnp.int32)
    return dst, wrow.reshape(cap, 1), eid, nvalid.reshape(1)


def _moe_layer(x, gain, mod, router_w, router_bias, w_gate, w_up, w_down, layer, seq,
               final_gain):
    wgu, wdb = _pack_expert_weights(w_gate, w_up, w_down, layer)
    h, idx, wgt = _route(x, gain, mod, router_w, router_bias, seq)
    dst, wrow, eid, nvalid = _dispatch_plan(idx, wgt)
    y_slots = _expert_ffn(eid, nvalid, dst, h, wgu, wdb, wrow, TOP_K * x.shape[0])
    return _combine(x, y_slots, mod, seq, final_gain)


def _moba_fox_mixer(x, gain, mod, w_in, forget_bias, w_out, batch, seq):
    n, d = x.shape
    mix = (MOBA_HEADS + FOX_HEADS) * HEAD_DIM
    qk_scale = HEAD_DIM ** -0.5 * LOG2E
    s1 = MOBA_HEADS * HEAD_DIM
    s2 = FOX_HEADS * HEAD_DIM
    col_scale = jnp.ones((3 * mix,), F32)
    col_scale = col_scale.at[:s1].set(qk_scale).at[3 * s1:3 * s1 + s2].set(qk_scale)
    w_main = (w_in[:, :3 * mix] * col_scale[None, :]).astype(BF16)
    w_forget = jnp.pad(w_in[:, 3 * mix:], ((0, 0), (0, LANES - FOX_HEADS))).astype(BF16)
    tn = 512
    proj, flog = _matmul(
        x, w_main, seq=seq, prologue="norm", gain=gain, mod=mod, shift_row=0, scale_row=1,
        epilogue="rot", rot_tables=_rope_tables(seq, ROT_DIM), rot_groups=range(tn // LANES),
        rot_tile_limit=2 * s1 // tn, rot_half=ROT_DIM // 2, w_side=w_forget, tn=tn,
        name="l0_in_proj")
    qx, kx = _fox_prep(flog, forget_bias, batch, seq)
    o_moba = _attention(proj, proj, proj, batch=batch, seq=seq, heads=MOBA_HEADS,
                        dk=HEAD_DIM, dv=HEAD_DIM, q_col=0, k_col=MOBA_HEADS,
                        v_col=2 * MOBA_HEADS, mode="moba", name="moba_attn")
    base = 3 * MOBA_HEADS
    o_fox = _attention(proj, proj, proj, batch=batch, seq=seq, heads=FOX_HEADS,
                       dk=HEAD_DIM, dv=HEAD_DIM, q_col=base, k_col=base + FOX_HEADS,
                       v_col=base + 2 * FOX_HEADS, mode="fox", qx=qx, kx=kx,
                       name="fox_attn")
    o = jnp.concatenate([o_moba, o_fox], axis=1)
    return _matmul(o, w_out.astype(BF16), seq=seq, out_dtype=F32, epilogue="residual",
                   mod=mod, x_res=x, gate_row=2, name="l0_out_proj")


def _mla_down_kernel(x_ref, gain_ref, mod_ref, wd_ref, qn_ref, kvn_ref, cos_ref, up_ref, dn_ref,
                     cq_ref, kin_ref, hb_ref):
    def store(r0, rows, h):
        hb_ref[pl.ds(r0, rows), :] = h.astype(BF16)
    _norm_mod_rows(x_ref, gain_ref, mod_ref, 0, 1, store)
    a = jnp.dot(hb_ref[...], wd_ref[...], preferred_element_type=F32)

    def rms(v, g):
        ms = jnp.mean(v * v, axis=-1, keepdims=True)
        return (v * lax.rsqrt(ms + NORM_EPS)) * g

    q_end = MLA_Q_LORA
    kv_end = MLA_Q_LORA + MLA_KV_LORA
    cq_ref[...] = rms(a[:, :q_end], qn_ref[...]).astype(BF16)
    kin_ref[:, :MLA_KV_LORA] = rms(a[:, q_end:kv_end], kvn_ref[...]).astype(BF16)
    kpe = _rope_group(a[:, kv_end:kv_end + LANES], cos_ref[...], up_ref[...], dn_ref[...],
                      MLA_ROPE_DIM // 2)
    kin_ref[:, MLA_KV_LORA:] = kpe.astype(BF16)


def _mla_down(x, gain, mod, w_down, q_norm, kv_norm, tables, seq):
    n, d = x.shape
    tm = _pick(seq, 512)
    tiles_per_seq = seq // tm
    wcols = w_down.shape[1]
    const = lambda i: (0, 0)
    return pl.pallas_call(
        _mla_down_kernel,
        out_shape=[jax.ShapeDtypeStruct((n, MLA_Q_LORA), BF16),
                   jax.ShapeDtypeStruct((n, MLA_KV_LORA + LANES), BF16)],
        grid=(n // tm,),
        in_specs=[pl.BlockSpec((tm, d), lambda i: (i, 0)),
                  pl.BlockSpec((1, d), const),
                  pl.BlockSpec((1, 6, d), lambda i: (i // tiles_per_seq, 0, 0)),
                  pl.BlockSpec((d, wcols), const),
                  pl.BlockSpec((1, MLA_Q_LORA), const),
                  pl.BlockSpec((1, MLA_KV_LORA), const)]
                 + [pl.BlockSpec((tm, LANES), lambda i: (i % tiles_per_seq, 0))] * 3,
        out_specs=[pl.BlockSpec((tm, MLA_Q_LORA), lambda i: (i, 0)),
                   pl.BlockSpec((tm, MLA_KV_LORA + LANES), lambda i: (i, 0))],
        scratch_shapes=[pltpu.VMEM((tm, d), BF16)],
        compiler_params=_cparams(("parallel",), 40),
        name="mla_down",
    )(x, gain.reshape(1, d), mod, w_down, q_norm.reshape(1, -1), kv_norm.reshape(1, -1), *tables)


def _mla_mixer(x, gain, mod, w_dq, q_norm, w_uq, w_dkv, kv_norm, w_ukv, w_out, batch, seq):
    n, d = x.shape
    dk = 2 * LANES
    pad_rope = LANES - MLA_ROPE_DIM
    qk_scale = (MLA_NOPE_DIM + MLA_ROPE_DIM) ** -0.5 * LOG2E
    tables = _rope_tables(seq, MLA_ROPE_DIM)
    w_down = jnp.concatenate(
        [w_dq, w_dkv, jnp.zeros((d, pad_rope), F32)], axis=1).astype(BF16)
    cq, kin = _mla_down(x, gain, mod, w_down, q_norm, kv_norm, tables, seq)

    w_q = (w_uq * qk_scale).reshape(MLA_Q_LORA, MLA_HEADS, MLA_NOPE_DIM + MLA_ROPE_DIM)
    w_q = jnp.pad(w_q, ((0, 0), (0, 0), (0, pad_rope))).reshape(MLA_Q_LORA, MLA_HEADS * dk)
    tn = 512
    q = _matmul(cq, w_q.astype(BF16), seq=seq, epilogue="rot", rot_tables=tables,
                rot_groups=range(1, tn // LANES, 2), rot_half=MLA_ROPE_DIM // 2, tn=tn,
                name="mla_q_proj")

    w_kv = w_ukv.reshape(MLA_KV_LORA, MLA_HEADS, MLA_NOPE_DIM + MLA_V_DIM)
    w_knope = jnp.pad(w_kv[:, :, :MLA_NOPE_DIM], ((0, LANES), (0, 0), (0, LANES)))
    rope_place = jnp.eye(LANES, dk, k=LANES, dtype=F32) * (
        jnp.arange(LANES) < MLA_ROPE_DIM).astype(F32)[:, None]
    rope_rows = jnp.concatenate([jnp.zeros((MLA_KV_LORA, dk), F32), rope_place], axis=0)
    w_k = w_knope + rope_rows[:, None, :]
    w_v = jnp.pad(w_kv[:, :, MLA_NOPE_DIM:], ((0, LANES), (0, 0), (0, 0)))
    w_kv_all = jnp.concatenate([w_k.reshape(MLA_KV_LORA + LANES, MLA_HEADS * dk),
                                w_v.reshape(MLA_KV_LORA + LANES, MLA_HEADS * MLA_V_DIM)], axis=1)
    kv = _matmul(kin, w_kv_all.astype(BF16), seq=seq, tn=tn, name="mla_kv_proj")

    o = _attention(q, kv, kv, batch=batch, seq=seq, heads=MLA_HEADS, dk=dk, dv=MLA_V_DIM,
                   q_col=0, k_col=0, v_col=MLA_HEADS * dk // MLA_V_DIM, mode="plain",
                   name="mla_attn")
    return _matmul(o, w_out.astype(BF16), seq=seq, out_dtype=F32, epilogue="residual",
                   mod=mod, x_res=x, gate_row=2, name="l1_out_proj")


def kernel(x, c, ada_w, ada_b, norm_mix, norm_ffn, ab_w_in, ab_forget_bias, ab_w_out, mla_w_dq,
           mla_q_norm, mla_w_uq, mla_w_dkv, mla_kv_norm, mla_w_ukv, mla_w_out, router_w,
           router_bias, exp_w_gate, exp_w_up, exp_w_down, final_norm):
    batch, seq, d = x.shape
    depth = ada_w.shape[0]
    mod = _adaln(c, ada_w, ada_b)
    xs = x.reshape(batch * seq, d)
    for layer in range(depth):
        if layer % 2 == 0:
            xs = _moba_fox_mixer(xs, norm_mix[layer], mod[layer], ab_w_in, ab_forget_bias,
                                 ab_w_out, batch, seq)
        else:
            xs = _mla_mixer(xs, norm_mix[layer], mod[layer], mla_w_dq, mla_q_norm, mla_w_uq,
                            mla_w_dkv, mla_kv_norm, mla_w_ukv, mla_w_out, batch, seq)
        final_gain = final_norm if layer == depth - 1 else None
        xs = _moe_layer(xs, norm_ffn[layer], mod[layer], router_w, router_bias,
                        exp_w_gate, exp_w_up, exp_w_down, layer, seq, final_gain)
    return xs.reshape(batch, seq, d)
```

```python
import functools
import math

import jax
import jax.numpy as jnp
from jax import lax
from jax.experimental import pallas as pl
from jax.experimental.pallas import tpu as pltpu

F32 = jnp.float32
BF16 = jnp.bfloat16

HEAD_DIM = 128
MOBA_HEADS = 8
FOX_HEADS = 8
ROT_DIM = HEAD_DIM // 4
ROPE_THETA = 500000.0
MOBA_BLOCK = 256
MOBA_TOPK = 3
MLA_HEADS = 16
MLA_Q_LORA = 512
MLA_KV_LORA = 512
MLA_NOPE_DIM = 128
MLA_ROPE_DIM = 64
MLA_V_DIM = 128
N_EXPERTS = 16
N_GROUPS = 4
EXPERTS_PER_GROUP = N_EXPERTS // N_GROUPS
TOP_K = 2
NORM_EPS = 1e-6

LANES = 128
V7X_VMEM_BYTES = 64 * 1024 * 1024
MIB = 1024 * 1024
LOG2E = math.log2(math.e)

NT_DIMS = (((1,), (1,)), ((), ()))


def _cparams(semantics, vmem_mib):
    assert vmem_mib * MIB < V7X_VMEM_BYTES
    return pltpu.CompilerParams(dimension_semantics=semantics,
                                vmem_limit_bytes=vmem_mib * MIB)


def _pick(n, pref):
    t = min(pref, n)
    while n % t:
        t //= 2
    return t


def _adaln_kernel(c_ref, w_ref, b_ref, o_ref):
    c = c_ref[...]
    ca = (c * jax.nn.sigmoid(c)).astype(BF16)
    acc = jnp.dot(ca, w_ref[0].astype(BF16), preferred_element_type=F32)
    o_ref[0] = acc + b_ref[0]


def _adaln(c, ada_w, ada_b):
    depth, d, n6 = ada_w.shape
    b = c.shape[0]
    rows = 8
    c_pad = jnp.pad(c, ((0, rows - b), (0, 0)))
    tn = _pick(n6, 1024)
    out = pl.pallas_call(
        _adaln_kernel,
        out_shape=jax.ShapeDtypeStruct((depth, rows, n6), F32),
        grid=(depth, n6 // tn),
        in_specs=[
            pl.BlockSpec((rows, d), lambda l, j: (0, 0)),
            pl.BlockSpec((1, d, tn), lambda l, j: (l, 0, j)),
            pl.BlockSpec((1, 1, tn), lambda l, j: (l, 0, j)),
        ],
        out_specs=pl.BlockSpec((1, rows, tn), lambda l, j: (l, 0, j)),
        compiler_params=_cparams(("parallel", "parallel"), 40),
        name="adaln_mod",
    )(c_pad, ada_w, ada_b.reshape(depth, 1, n6))
    return out[:, :b, :].reshape(depth, b, 6, d)


NORM_ROWS = 256


def _norm_mod_rows(x_ref, gain_ref, mod_ref, shift_row, scale_row, store):
    tm = x_ref.shape[0]
    rows = min(NORM_ROWS, tm)
    gain = gain_ref[...]
    scale1p = 1.0 + mod_ref[0, scale_row:scale_row + 1, :]
    shift = mod_ref[0, shift_row:shift_row + 1, :]

    def body(c, carry):
        r0 = pl.multiple_of(c * rows, rows)
        x = x_ref[pl.ds(r0, rows), :]
        ms = jnp.mean(x * x, axis=-1, keepdims=True)
        y = x * lax.rsqrt(ms + NORM_EPS)
        h = (y * gain) * scale1p + shift
        store(r0, rows, h)
        return carry

    lax.fori_loop(0, tm // rows, body, 0)


def _rope_group(a, cos_t, sin_up, sin_dn, half):
    up = pltpu.roll(a, LANES - half, 1)
    dn = pltpu.roll(a, half, 1)
    return a * cos_t + up * sin_up + dn * sin_dn


def _rope_tables(seq, rot_dim):
    half = rot_dim // 2
    inv_freq = ROPE_THETA ** (-jnp.arange(half, dtype=F32) / half)
    ang = jnp.arange(seq, dtype=F32)[:, None] * inv_freq[None, :]
    cos, sin = jnp.cos(ang), jnp.sin(ang)
    zeros = jnp.zeros((seq, LANES - rot_dim), F32)
    zh = jnp.zeros((seq, half), F32)
    cos_t = jnp.concatenate([cos, cos, jnp.ones((seq, LANES - rot_dim), F32)], axis=1)
    sin_up = jnp.concatenate([-sin, zh, zeros], axis=1)
    sin_dn = jnp.concatenate([zh, sin, zeros], axis=1)
    return cos_t, sin_up, sin_dn


def _mm_kernel(*refs, prologue, epilogue, rot_groups, rot_tile_limit, rot_half,
               shift_row, scale_row, gate_row, has_side):
    it = iter(refs)
    a_ref = next(it)
    if prologue == "norm":
        gain_ref = next(it)
    if prologue == "norm" or epilogue == "residual":
        mod_ref = next(it)
    w_ref = next(it)
    if has_side:
        wside_ref = next(it)
    if epilogue == "rot":
        cos_ref, up_ref, dn_ref = next(it), next(it), next(it)
    if epilogue == "residual":
        x_ref = next(it)
    o_ref = next(it)
    if has_side:
        side_ref = next(it)
    if prologue == "norm":
        hb_ref = next(it)

    j = pl.program_id(1)

    if prologue == "norm":
        @pl.when(j == 0)
        def _():
            def store(r0, rows, h):
                hb_ref[pl.ds(r0, rows), :] = h.astype(BF16)
            _norm_mod_rows(a_ref, gain_ref, mod_ref, shift_row, scale_row, store)
            if has_side:
                side_ref[...] = jnp.dot(hb_ref[...], wside_ref[...],
                                        preferred_element_type=F32)
        lhs = hb_ref[...]
    else:
        lhs = a_ref[...]

    acc = jnp.dot(lhs, w_ref[...], preferred_element_type=F32)

    if epilogue == "rot":
        def rotated():
            groups = []
            for g in range(acc.shape[1] // LANES):
                blk = acc[:, g * LANES:(g + 1) * LANES]
                if g in rot_groups:
                    blk = _rope_group(blk, cos_ref[...], up_ref[...], dn_ref[...], rot_half)
                groups.append(blk)
            return jnp.concatenate(groups, axis=1)

        if rot_tile_limit is None:
            o_ref[...] = rotated().astype(o_ref.dtype)
        else:
            @pl.when(j < rot_tile_limit)
            def _():
                o_ref[...] = rotated().astype(o_ref.dtype)

            @pl.when(j >= rot_tile_limit)
            def _():
                o_ref[...] = acc.astype(o_ref.dtype)
    elif epilogue == "residual":
        gate = mod_ref[0, gate_row:gate_row + 1, :]
        o_ref[...] = x_ref[...] + gate * acc
    else:
        o_ref[...] = acc.astype(o_ref.dtype)


def _matmul(a, w, *, seq, out_dtype=BF16, tm=1024, tn=512, prologue="none", gain=None,
            mod=None, shift_row=0, scale_row=1, epilogue="none", rot_tables=None,
            rot_groups=(), rot_tile_limit=None, rot_half=0, x_res=None, gate_row=2,
            w_side=None, name="matmul"):
    m, k = a.shape
    n = w.shape[1]
    tm = _pick(seq, tm)
    tn = _pick(n, tn)
    tiles_per_seq = seq // tm
    has_side = w_side is not None
    in_specs = [pl.BlockSpec((tm, k), lambda i, j: (i, 0))]
    args = [a]
    if prologue == "norm":
        in_specs.append(pl.BlockSpec((1, k), lambda i, j: (0, 0)))
        args.append(gain.reshape(1, k))
    if prologue == "norm":
        in_specs.append(pl.BlockSpec((1, 6, k), lambda i, j: (i // tiles_per_seq, 0, 0)))
        args.append(mod)
    elif epilogue == "residual":
        in_specs.append(pl.BlockSpec((1, 6, tn), lambda i, j: (i // tiles_per_seq, 0, j)))
        args.append(mod)
    in_specs.append(pl.BlockSpec((k, tn), lambda i, j: (0, j)))
    args.append(w)
    if has_side:
        in_specs.append(pl.BlockSpec((k, LANES), lambda i, j: (0, 0)))
        args.append(w_side)
    if epilogue == "rot":
        for t in rot_tables:
            in_specs.append(pl.BlockSpec((tm, LANES), lambda i, j: (i % tiles_per_seq, 0)))
            args.append(t)
    if epilogue == "residual":
        in_specs.append(pl.BlockSpec((tm, tn), lambda i, j: (i, j)))
        args.append(x_res)
    out_shape = [jax.ShapeDtypeStruct((m, n), out_dtype)]
    out_specs = [pl.BlockSpec((tm, tn), lambda i, j: (i, j))]
    if has_side:
        out_shape.append(jax.ShapeDtypeStruct((m, LANES), F32))
        out_specs.append(pl.BlockSpec((tm, LANES), lambda i, j: (i, 0)))
    scratch = [pltpu.VMEM((tm, k), BF16)] if prologue == "norm" else []
    kernel = functools.partial(
        _mm_kernel, prologue=prologue, epilogue=epilogue, rot_groups=tuple(rot_groups),
        rot_tile_limit=rot_tile_limit, rot_half=rot_half, shift_row=shift_row,
        scale_row=scale_row, gate_row=gate_row, has_side=has_side)
    outs = pl.pallas_call(
        kernel,
        out_shape=out_shape,
        grid=(m // tm, n // tn),
        in_specs=in_specs,
        out_specs=out_specs,
        scratch_shapes=scratch,
        compiler_params=_cparams(("parallel", "arbitrary"), 48),
        name=name,
    )(*args)
    return outs if has_side else outs[0]


def _fox_prep_kernel(flog_ref, bias_ref, qx_ref, kx_ref, carry_ref):
    tc = flog_ref.shape[0]

    @pl.when(pl.program_id(1) == 0)
    def _():
        carry_ref[...] = jnp.zeros_like(carry_ref)

    z = flog_ref[...] + bias_ref[...]
    logf = jnp.minimum(z, 0.0) - jnp.log1p(jnp.exp(-jnp.abs(z)))
    row = lax.broadcasted_iota(jnp.int32, (tc, tc), 0)
    col = lax.broadcasted_iota(jnp.int32, (tc, tc), 1)
    tri = jnp.where(col <= row, 1.0, 0.0).astype(F32)
    csum = jnp.dot(tri, logf, preferred_element_type=F32,
                   precision=lax.Precision.HIGHEST) + carry_ref[...]
    carry_ref[...] = csum[tc - 1:tc, :]
    csum = csum * LOG2E
    lane = lax.broadcasted_iota(jnp.int32, (tc, LANES), 1)
    one = jnp.ones((tc, LANES), F32)
    zero = jnp.zeros((tc, LANES), F32)
    for h in range(FOX_HEADS):
        colv = jnp.broadcast_to(csum[:, h:h + 1], (tc, LANES))
        hi = colv.astype(BF16).astype(F32)
        r1 = colv - hi
        mid = r1.astype(BF16).astype(F32)
        lo = r1 - mid
        qv = jnp.where(lane == 0, hi, jnp.where(lane == 1, mid, jnp.where(
            lane == 2, lo, jnp.where(lane < 6, one, zero))))
        kv = jnp.where(lane < 3, one, jnp.where(lane == 3, -hi, jnp.where(
            lane == 4, -mid, jnp.where(lane == 5, -lo, zero))))
        qx_ref[:, h * LANES:(h + 1) * LANES] = qv.astype(BF16)
        kx_ref[:, h * LANES:(h + 1) * LANES] = kv.astype(BF16)


def _fox_prep(flog, forget_bias, batch, seq):
    n = flog.shape[0]
    tc = _pick(seq, 256)
    bias = jnp.pad(forget_bias, (0, LANES - FOX_HEADS)).reshape(1, LANES)
    width = FOX_HEADS * LANES
    steps = seq // tc
    return pl.pallas_call(
        _fox_prep_kernel,
        out_shape=[jax.ShapeDtypeStruct((n, width), BF16)] * 2,
        grid=(batch, steps),
        in_specs=[pl.BlockSpec((tc, LANES), lambda b, s: (b * steps + s, 0)),
                  pl.BlockSpec((1, LANES), lambda b, s: (0, 0))],
        out_specs=[pl.BlockSpec((tc, width), lambda b, s: (b * steps + s, 0))] * 2,
        scratch_shapes=[pltpu.VMEM((1, LANES), F32)],
        compiler_params=_cparams(("parallel", "arbitrary"), 32),
        name="fox_prep",
    )(flog, bias)


ATTN_TILE = 512
ATTN_HEADS_PER_STEP = 4


M_INIT = -1e30


def _attn_kernel(qt_ref, kt_ref, *refs, mode, tq, dk, dv):
    it = iter(refs)
    q_ref, k_ref, v_ref = next(it), next(it), next(it)
    if mode == "fox":
        qx_ref, kx_ref = next(it), next(it)
    if mode == "moba":
        kmean_ref = next(it)
    o_ref = next(it)
    m_ref, l_ref, acc_ref = next(it), next(it), next(it)
    if mode == "moba":
        sel_ref = next(it)

    p_id = pl.program_id(2)
    i, j = qt_ref[p_id], kt_ref[p_id]
    neg_inf = jnp.float32(-jnp.inf)
    heads = q_ref.shape[1] // dk
    blocks_per_tile = tq // MOBA_BLOCK
    block_shift = MOBA_BLOCK.bit_length() - 1

    def q_of(h):
        q = q_ref[:, h * dk:(h + 1) * dk]
        if mode == "fox":
            q = jnp.concatenate([q, qx_ref[:, h * LANES:(h + 1) * LANES]], axis=1)
        return q

    def scores(h):
        k = k_ref[:, h * dk:(h + 1) * dk]
        if mode == "fox":
            k = jnp.concatenate([k, kx_ref[:, h * LANES:(h + 1) * LANES]], axis=1)
        return lax.dot_general(q_of(h), k, NT_DIMS, preferred_element_type=F32)

    def block_hits(h):
        lane = lax.broadcasted_iota(jnp.int32, (tq, LANES), 1)
        sel = sel_ref[h]
        return [jnp.broadcast_to(
            jnp.max(jnp.where(lane == j * blocks_per_tile + c, sel, 0.0), axis=1, keepdims=True),
            (tq, LANES)) > 0.0 for c in range(blocks_per_tile)]

    def masked(s, allowed_of_group):
        groups = [jnp.where(allowed_of_group(g), s[:, g * LANES:(g + 1) * LANES], neg_inf)
                  for g in range(tq // LANES)]
        return jnp.concatenate(groups, axis=1)

    def update(h, s):
        m_old = m_ref[h]
        row_max = jnp.broadcast_to(jnp.max(s, axis=1, keepdims=True), m_old.shape)
        m_new = jnp.maximum(m_old, row_max)
        alpha = jnp.exp2(m_old - m_new)
        p = jnp.exp2(s - jnp.concatenate([m_new] * (s.shape[1] // LANES), axis=1))
        row_sum = jnp.broadcast_to(jnp.sum(p, axis=1, keepdims=True), m_old.shape)
        l_ref[h] = alpha * l_ref[h] + row_sum
        acc_ref[h] = alpha * acc_ref[h] + jnp.dot(
            p.astype(BF16), v_ref[:, h * dv:(h + 1) * dv], preferred_element_type=F32)
        m_ref[h] = m_new

    @pl.when(j == 0)
    def _():
        m_ref[...] = jnp.full_like(m_ref, M_INIT)
        l_ref[...] = jnp.zeros_like(l_ref)
        acc_ref[...] = jnp.zeros_like(acc_ref)
        if mode == "moba":
            nb = kmean_ref.shape[0]
            lane = lax.broadcasted_iota(jnp.int32, (tq, LANES), 1)
            lane_f = lane.astype(F32)
            row_block = jnp.right_shift(lax.broadcasted_iota(jnp.int32, (tq, LANES), 0),
                                        block_shift)
            past = lane < i * blocks_per_tile + row_block
            for h in range(heads):
                kmean = jnp.concatenate(
                    [kmean_ref[:, h * dk:(h + 1) * dk], jnp.zeros((LANES - nb, dk), F32)], axis=0)
                gate = lax.dot_general(q_of(h).astype(F32), kmean, NT_DIMS,
                                       preferred_element_type=F32,
                                       precision=lax.Precision.HIGHEST)
                g = jnp.where(past, gate, neg_inf)
                chosen = jnp.zeros(gate.shape, F32)
                for _ in range(min(MOBA_TOPK, nb - 1)):
                    mx = jnp.broadcast_to(jnp.max(g, axis=1, keepdims=True), g.shape)
                    first = jnp.broadcast_to(
                        jnp.min(jnp.where(g == mx, lane_f, float(LANES)), axis=1, keepdims=True),
                        g.shape)
                    pick = lane_f == first
                    chosen = jnp.where(jnp.logical_and(pick, past), 1.0, chosen)
                    g = jnp.where(pick, neg_inf, g)
                sel_ref[h] = chosen

    groups_per_block = MOBA_BLOCK // LANES

    @pl.when(j < i)
    def _():
        for h in range(heads):
            s = scores(h)
            if mode == "moba":
                hits = block_hits(h)
                s = masked(s, lambda g: hits[g // groups_per_block])
            update(h, s)

    @pl.when(j == i)
    def _():
        row = lax.broadcasted_iota(jnp.int32, (tq, LANES), 0)
        lane = lax.broadcasted_iota(jnp.int32, (tq, LANES), 1)
        row_blk = jnp.right_shift(row, block_shift)
        for h in range(heads):
            s = scores(h)
            if mode == "moba":
                hits = block_hits(h)

                def allowed(g):
                    col_blk = g // groups_per_block
                    own = jnp.logical_and(row_blk == col_blk, lane + g * LANES <= row)
                    return jnp.logical_or(own, jnp.logical_and(row_blk > col_blk, hits[col_blk]))
            else:
                def allowed(g):
                    return lane + g * LANES <= row
            update(h, masked(s, allowed))
            o_ref[:, h * dv:(h + 1) * dv] = (acc_ref[h] / l_ref[h]).astype(o_ref.dtype)


def _attention(q_arr, k_arr, v_arr, *, batch, seq, heads, dk, dv, q_col, k_col, v_col, mode,
               qx=None, kx=None, kmean=None, name="attn"):
    n = batch * seq
    tq = _pick(seq, ATTN_TILE)
    g = ATTN_HEADS_PER_STEP
    assert tq % MOBA_BLOCK == 0 and heads % g == 0
    assert q_col % g == 0 and k_col % g == 0 and v_col % g == 0
    nq = seq // tq
    pairs = [(i, j) for i in range(nq) for j in range(i + 1)]
    q_tile = jnp.array([p[0] for p in pairs], jnp.int32)
    k_tile = jnp.array([p[1] for p in pairs], jnp.int32)

    def q_rows(b, h, p, qt, kt):
        return b * nq + qt[p]

    def k_rows(b, h, p, qt, kt):
        return b * nq + kt[p]

    in_specs = [
        pl.BlockSpec((tq, g * dk), lambda b, h, p, qt, kt: (q_rows(b, h, p, qt, kt), q_col // g + h)),
        pl.BlockSpec((tq, g * dk), lambda b, h, p, qt, kt: (k_rows(b, h, p, qt, kt), k_col // g + h)),
        pl.BlockSpec((tq, g * dv), lambda b, h, p, qt, kt: (k_rows(b, h, p, qt, kt), v_col // g + h)),
    ]
    args = [q_arr, k_arr, v_arr]
    if mode == "fox":
        in_specs += [
            pl.BlockSpec((tq, g * LANES), lambda b, h, p, qt, kt: (q_rows(b, h, p, qt, kt), h)),
            pl.BlockSpec((tq, g * LANES), lambda b, h, p, qt, kt: (k_rows(b, h, p, qt, kt), h))]
        args += [qx, kx]
    if mode == "moba":
        nb = seq // MOBA_BLOCK
        in_specs.append(pl.BlockSpec((nb, g * dk), lambda b, h, p, qt, kt: (b, k_col // g + h)))
        args.append(kmean)
    assert dv == LANES
    scratch = [pltpu.VMEM((g, tq, LANES), F32), pltpu.VMEM((g, tq, LANES), F32),
               pltpu.VMEM((g, tq, dv), F32)]
    if mode == "moba":
        scratch.append(pltpu.VMEM((g, tq, LANES), F32))
    return pl.pallas_call(
        functools.partial(_attn_kernel, mode=mode, tq=tq, dk=dk, dv=dv),
        out_shape=jax.ShapeDtypeStruct((n, heads * dv), BF16),
        grid_spec=pltpu.PrefetchScalarGridSpec(
            num_scalar_prefetch=2,
            grid=(batch, heads // g, len(pairs)),
            in_specs=in_specs,
            out_specs=pl.BlockSpec((tq, g * dv),
                                   lambda b, h, p, qt, kt: (q_rows(b, h, p, qt, kt), h)),
            scratch_shapes=scratch,
        ),
        compiler_params=_cparams(("parallel", "parallel", "arbitrary"), 40),
        name=name,
    )(q_tile, k_tile, *args)


def _block_means_kernel(k_ref, o_ref):
    for b in range(o_ref.shape[0]):
        kb = k_ref[b * MOBA_BLOCK:(b + 1) * MOBA_BLOCK, :].astype(F32)
        o_ref[b:b + 1, :] = jnp.mean(kb, axis=0, keepdims=True)


def _block_means(arr, col0, cols):
    n = arr.shape[0]
    blocks = 8
    rows = blocks * MOBA_BLOCK
    assert n % rows == 0 and col0 % cols == 0
    return pl.pallas_call(
        _block_means_kernel,
        out_shape=jax.ShapeDtypeStruct((n // MOBA_BLOCK, col0 + cols), F32),
        grid=(n // rows,),
        in_specs=[pl.BlockSpec((rows, cols), lambda i: (i, col0 // cols))],
        out_specs=pl.BlockSpec((blocks, cols), lambda i: (i, col0 // cols)),
        compiler_params=_cparams(("parallel",), 32),
        name="moba_block_means",
    )(arr)


def _top2_of4(vals):
    a, b, c, d = vals
    hi1, lo1 = jnp.maximum(a, b), jnp.minimum(a, b)
    hi2, lo2 = jnp.maximum(c, d), jnp.minimum(c, d)
    return jnp.maximum(hi1, hi2) + jnp.maximum(jnp.minimum(hi1, hi2), jnp.maximum(lo1, lo2))


def _route_kernel(x_ref, gain_ref, mod_ref, rwt_ref, rbias_ref, h_ref, idx_ref, wgt_ref):
    def store(r0, rows, h):
        h_ref[pl.ds(r0, rows), :] = h
    _norm_mod_rows(x_ref, gain_ref, mod_ref, 3, 4, store)

    logits = lax.dot_general(rwt_ref[...], h_ref[...], NT_DIMS, preferred_element_type=F32,
                             precision=lax.Precision.HIGHEST)
    score = jax.nn.sigmoid(logits)
    biased = score + rbias_ref[...]
    rows_b = [biased[e:e + 1, :] for e in range(N_EXPERTS)]
    rows_s = [score[e:e + 1, :] for e in range(N_EXPERTS)]
    grp = [_top2_of4(rows_b[g * EXPERTS_PER_GROUP:(g + 1) * EXPERTS_PER_GROUP])
           for g in range(N_GROUPS)]
    best, g_sel = grp[0], jnp.zeros_like(grp[0], dtype=jnp.int32)
    for g in range(1, N_GROUPS):
        better = grp[g] > best
        best = jnp.where(better, grp[g], best)
        g_sel = jnp.where(better, g, g_sel)

    def in_group(rows, r):
        out = rows[r]
        for g in range(1, N_GROUPS):
            out = jnp.where(g_sel == g, rows[g * EXPERTS_PER_GROUP + r], out)
        return out

    cb = [in_group(rows_b, r) for r in range(EXPERTS_PER_GROUP)]
    cs = [in_group(rows_s, r) for r in range(EXPERTS_PER_GROUP)]
    neg_inf = jnp.float32(-jnp.inf)
    picks = []
    for _ in range(TOP_K):
        bv, bi, bs = cb[0], jnp.zeros_like(g_sel), cs[0]
        for r in range(1, EXPERTS_PER_GROUP):
            better = cb[r] > bv
            bv = jnp.where(better, cb[r], bv)
            bi = jnp.where(better, r, bi)
            bs = jnp.where(better, cs[r], bs)
        picks.append((bi, bs))
        cb = [jnp.where(bi == r, neg_inf, cb[r]) for r in range(EXPERTS_PER_GROUP)]
    (i0, s0), (i1, s1) = picks
    total = s0 + s1
    idx_ref[0:1, :] = g_sel * EXPERTS_PER_GROUP + i0
    idx_ref[1:2, :] = g_sel * EXPERTS_PER_GROUP + i1
    wgt_ref[0:1, :] = s0 / total
    wgt_ref[1:2, :] = s1 / total


def _route(x, gain, mod, router_w, router_bias, seq):
    n, d = x.shape
    tm = _pick(seq, 512)
    tiles_per_seq = seq // tm
    return pl.pallas_call(
        _route_kernel,
        out_shape=[jax.ShapeDtypeStruct((n, d), F32),
                   jax.ShapeDtypeStruct((TOP_K, n), jnp.int32),
                   jax.ShapeDtypeStruct((TOP_K, n), F32)],
        grid=(n // tm,),
        in_specs=[pl.BlockSpec((tm, d), lambda i: (i, 0)),
                  pl.BlockSpec((1, d), lambda i: (0, 0)),
                  pl.BlockSpec((1, 6, d), lambda i: (i // tiles_per_seq, 0, 0)),
                  pl.BlockSpec((N_EXPERTS, d), lambda i: (0, 0)),
                  pl.BlockSpec((N_EXPERTS, 1), lambda i: (0, 0))],
        out_specs=[pl.BlockSpec((tm, d), lambda i: (i, 0)),
                   pl.BlockSpec((TOP_K, tm), lambda i: (0, i)),
                   pl.BlockSpec((TOP_K, tm), lambda i: (0, i))],
        compiler_params=_cparams(("parallel",), 32),
        name="moe_route",
    )(x, gain.reshape(1, d), mod, router_w.T, router_bias.reshape(N_EXPERTS, 1))


GMM_ROWS = 512
GMM_UP_COLS = 128
GMM_DOWN_COLS = 512
GMM_ISSUE_STEPS = 8
PACK_ROWS = 256
DMA_UNROLL = 8


def _pack_kernel(wg_ref, wu_ref, wd_ref, wgu_ref, wdb_ref):
    n_up = wgu_ref.shape[1]
    for f in range(n_up):
        cols = slice(f * GMM_UP_COLS, (f + 1) * GMM_UP_COLS)
        wgu_ref[0, f, :, :GMM_UP_COLS] = wg_ref[0, 0, :, cols].astype(BF16)
        wgu_ref[0, f, :, GMM_UP_COLS:] = wu_ref[0, 0, :, cols].astype(BF16)
    wdb_ref[0] = wd_ref[0, 0].astype(BF16)


def _pack_expert_weights(w_gate, w_up, w_down, layer):
    _, n_exp, d, d_exp = w_gate.shape
    n_up = d_exp // GMM_UP_COLS
    steps = d // PACK_ROWS
    down_rows = d_exp // steps
    assert d_exp % GMM_UP_COLS == 0 and d % PACK_ROWS == 0 and down_rows % 16 == 0
    up_spec = pl.BlockSpec((1, 1, PACK_ROWS, d_exp), lambda e, k: (layer, e, k, 0))
    return pl.pallas_call(
        _pack_kernel,
        out_shape=[jax.ShapeDtypeStruct((n_exp, n_up, d, 2 * GMM_UP_COLS), BF16),
                   jax.ShapeDtypeStruct((n_exp, d_exp, d), BF16)],
        grid=(n_exp, steps),
        in_specs=[up_spec, up_spec,
                  pl.BlockSpec((1, 1, down_rows, d), lambda e, k: (layer, e, k, 0))],
        out_specs=[pl.BlockSpec((1, n_up, PACK_ROWS, 2 * GMM_UP_COLS), lambda e, k: (e, 0, k, 0)),
                   pl.BlockSpec((1, down_rows, d), lambda e, k: (e, k, 0))],
        compiler_params=_cparams(("parallel", "parallel"), 32),
        name="moe_pack_weights",
    )(w_gate, w_up, w_down)


def _ffn_kernel(eid_ref, nvalid_ref, dst_ref, h_hbm, wgu_ref, wd_ref, wrow_ref, y_hbm,
                xbuf_ref, hb_ref, act_ref, ystage_ref, gsem, ssem):
    t, f = pl.program_id(0), pl.program_id(1)
    rows = hb_ref.shape[0]
    n_up = act_ref.shape[0]
    n_tokens = h_hbm.shape[0]
    per_step = rows // GMM_ISSUE_STEPS
    nv = nvalid_ref[0]
    valid = t < nv
    slot = lax.rem(t, 2)

    def dst_of(tile, r):
        return dst_ref[(tile + 1) * rows + r]

    def token_of(dst_row):
        if n_tokens & (n_tokens - 1) == 0:
            return jnp.bitwise_and(dst_row, n_tokens - 1)
        return lax.rem(dst_row, n_tokens)

    def gather_row(buf, r, src_row):
        return pltpu.make_async_copy(h_hbm.at[pl.ds(src_row, 1), :],
                                     xbuf_ref.at[buf, pl.ds(r, 1), :], gsem.at[buf])

    def scatter_row(r, dst_row):
        return pltpu.make_async_copy(ystage_ref.at[pl.ds(r, 1), :],
                                     y_hbm.at[pl.ds(dst_row, 1), :], ssem)

    def for_rows(fn):
        def body(r, carry):
            fn(r)
            return carry
        lax.fori_loop(0, rows, body, 0, unroll=DMA_UNROLL)

    @pl.when(jnp.logical_and(t == 0, f == 0))
    def _():
        ystage_ref[...] = jnp.zeros_like(ystage_ref)
        for_rows(lambda r: gather_row(0, r, token_of(dst_of(0, r))).start())

    @pl.when(jnp.logical_and(t <= nv, f == 0))
    def _():
        for_rows(lambda r: gather_row(slot, r, 0).wait())

    @pl.when(jnp.logical_and(t == nv, f == 0))
    def _():
        for_rows(lambda r: scatter_row(r, dst_of(t - 1, r)).start())

    @pl.when(jnp.logical_and(valid, f == 0))
    def _():
        hb_ref[...] = xbuf_ref[slot].astype(BF16)

    def up_step():
        gu = jnp.dot(hb_ref[...], wgu_ref[0, 0], preferred_element_type=F32)
        g, u = gu[:, :GMM_UP_COLS], gu[:, GMM_UP_COLS:]
        act_ref[f] = ((g * jax.nn.sigmoid(g)) * u).astype(BF16)

    @pl.when(jnp.logical_and(valid, f < GMM_ISSUE_STEPS))
    def _():
        for k in range(per_step):
            r = f * per_step + k
            gather_row(1 - slot, r, token_of(dst_of(t + 1, r))).start()
            scatter_row(r, dst_of(t - 1, r)).start()
        up_step()

    @pl.when(jnp.logical_and(valid, jnp.logical_and(f >= GMM_ISSUE_STEPS, f < n_up)))
    def _():
        up_step()

    @pl.when(jnp.logical_and(t <= nv, f == n_up))
    def _():
        for_rows(lambda r: scatter_row(r, 0).wait())

    @pl.when(jnp.logical_and(valid, f == n_up))
    def _():
        act = jnp.concatenate([act_ref[k] for k in range(n_up)], axis=1)
        y = jnp.dot(act, wd_ref[0], preferred_element_type=F32)
        ystage_ref[...] = y * wrow_ref[...]


def _expert_ffn(eid, nvalid, dst, h, wgu, wdb, wrow, n_slots):
    n_tiles = dst.shape[0] // GMM_ROWS - 1
    d = h.shape[1]
    n_exp, n_up, _, up_cols = wgu.shape
    d_exp = wdb.shape[1]
    assert GMM_ROWS % GMM_ISSUE_STEPS == 0 and n_up >= GMM_ISSUE_STEPS

    def up_block(t, f, e, nv, ds):
        return (e[t], jnp.where(t < nv[0], jnp.minimum(f, n_up - 1), n_up - 1), 0, 0)

    return pl.pallas_call(
        _ffn_kernel,
        out_shape=jax.ShapeDtypeStruct((n_slots + GMM_ROWS, d), F32),
        grid_spec=pltpu.PrefetchScalarGridSpec(
            num_scalar_prefetch=3,
            grid=(n_tiles, n_up + 1),
            in_specs=[
                pl.BlockSpec(memory_space=pl.ANY),
                pl.BlockSpec((1, 1, d, up_cols), up_block),
                pl.BlockSpec((1, d_exp, d), lambda t, f, e, nv, ds: (e[t], 0, 0)),
                pl.BlockSpec((GMM_ROWS, 1), lambda t, f, e, nv, ds: (t, 0)),
            ],
            out_specs=pl.BlockSpec(memory_space=pl.ANY),
            scratch_shapes=[pltpu.VMEM((2, GMM_ROWS, d), F32),
                            pltpu.VMEM((GMM_ROWS, d), BF16),
                            pltpu.VMEM((n_up, GMM_ROWS, GMM_UP_COLS), BF16),
                            pltpu.VMEM((GMM_ROWS, d), F32),
                            pltpu.SemaphoreType.DMA((2,)),
                            pltpu.SemaphoreType.DMA],
        ),
        compiler_params=_cparams(("arbitrary", "arbitrary"), 48),
        name="moe_ffn",
    )(eid, nvalid, dst, h, wgu, wdb, wrow)


def _combine_kernel(x_ref, y0_ref, y1_ref, mod_ref, *rest, final):
    if final:
        gain_ref, o_ref = rest
    else:
        (o_ref,) = rest
    rows = min(NORM_ROWS, x_ref.shape[0])
    gate = mod_ref[0, 5:6, :]

    def body(c, carry):
        r0 = pl.multiple_of(c * rows, rows)
        y = y0_ref[pl.ds(r0, rows), :] + y1_ref[pl.ds(r0, rows), :]
        out = x_ref[pl.ds(r0, rows), :] + gate * y
        if final:
            ms = jnp.mean(out * out, axis=-1, keepdims=True)
            out = (out * lax.rsqrt(ms + NORM_EPS)) * gain_ref[...]
        o_ref[pl.ds(r0, rows), :] = out
        return carry

    lax.fori_loop(0, x_ref.shape[0] // rows, body, 0)


def _combine(x, y_slots, mod, seq, final_gain=None):
    n, d = x.shape
    tm = _pick(seq, 512)
    tiles_per_seq = seq // tm
    tiles = n // tm
    final = final_gain is not None
    in_specs = [pl.BlockSpec((tm, d), lambda i: (i, 0)),
                pl.BlockSpec((tm, d), lambda i: (i, 0)),
                pl.BlockSpec((tm, d), lambda i: (tiles + i, 0)),
                pl.BlockSpec((1, 6, d), lambda i: (i // tiles_per_seq, 0, 0))]
    args = [x, y_slots, y_slots, mod]
    if final:
        in_specs.append(pl.BlockSpec((1, d), lambda i: (0, 0)))
        args.append(final_gain.reshape(1, d))
    return pl.pallas_call(
        functools.partial(_combine_kernel, final=final),
        out_shape=jax.ShapeDtypeStruct((n, d), F32),
        grid=(tiles,),
        in_specs=in_specs,
        out_specs=pl.BlockSpec((tm, d), lambda i: (i, 0)),
        compiler_params=_cparams(("parallel",), 48),
        name="moe_combine",
    )(*args)


def _dispatch_plan(idx, wgt):
    n = idx.shape[1]
    slots = TOP_K * n
    cap = slots + N_EXPERTS * GMM_ROWS
    expert = idx.reshape(slots)
    onehot = (expert[:, None] == jnp.arange(N_EXPERTS)[None, :]).astype(jnp.int32)
    csum = jnp.cumsum(onehot, axis=0)
    rank = jnp.sum(csum * onehot, axis=1) - 1
    counts = csum[-1]
    padded = -(-counts // GMM_ROWS) * GMM_ROWS
    ends = jnp.cumsum(padded)
    pos = (ends - padded)[expert] + rank
    slot_ids = jnp.arange(slots, dtype=jnp.int32)
    dummy = slots + jnp.arange(cap + GMM_ROWS, dtype=jnp.int32) % GMM_ROWS
    dst = dummy.at[pos + GMM_ROWS].set(slot_ids)
    wrow = jnp.zeros((cap,), F32).at[pos].set(wgt.reshape(slots))
    nvalid = (ends[-1] // GMM_ROWS).astype(jnp.int32)
    tile_start = jnp.arange(cap // GMM_ROWS, dtype=jnp.int32) * GMM_ROWS
    tile_start = jnp.minimum(tile_start, ends[-1] - GMM_ROWS)
    eid = jnp.sum(tile_start[:, None] >= ends[None, :], axis=1).astype(jnp.int32)
    return dst, wrow.reshape(cap, 1), eid, nvalid.reshape(1)


def _moe_layer(x, gain, mod, router_w, router_bias, w_gate, w_up, w_down, layer, seq,
               final_gain):
    wgu, wdb = _pack_expert_weights(w_gate, w_up, w_down, layer)
    h, idx, wgt = _route(x, gain, mod, router_w, router_bias, seq)
    dst, wrow, eid, nvalid = _dispatch_plan(idx, wgt)
    y_slots = _expert_ffn(eid, nvalid, dst, h, wgu, wdb, wrow, TOP_K * x.shape[0])
    return _combine(x, y_slots, mod, seq, final_gain)


def _moba_fox_mixer(x, gain, mod, w_in, forget_bias, w_out, batch, seq):
    n, d = x.shape
    mix = (MOBA_HEADS + FOX_HEADS) * HEAD_DIM
    qk_scale = HEAD_DIM ** -0.5 * LOG2E
    s1 = MOBA_HEADS * HEAD_DIM
    s2 = FOX_HEADS * HEAD_DIM
    col_scale = jnp.ones((3 * mix,), F32)
    col_scale = col_scale.at[:s1].set(qk_scale).at[3 * s1:3 * s1 + s2].set(qk_scale)
    w_main = (w_in[:, :3 * mix] * col_scale[None, :]).astype(BF16)
    w_forget = jnp.pad(w_in[:, 3 * mix:], ((0, 0), (0, LANES - FOX_HEADS))).astype(BF16)
    tn = 512
    proj, flog = _matmul(
        x, w_main, seq=seq, prologue="norm", gain=gain, mod=mod, shift_row=0, scale_row=1,
        epilogue="rot", rot_tables=_rope_tables(seq, ROT_DIM), rot_groups=range(tn // LANES),
        rot_tile_limit=2 * s1 // tn, rot_half=ROT_DIM // 2, w_side=w_forget, tn=tn,
        name="l0_in_proj")
    qx, kx = _fox_prep(flog, forget_bias, batch, seq)
    o_moba = _attention(proj, proj, proj, batch=batch, seq=seq, heads=MOBA_HEADS,
                        dk=HEAD_DIM, dv=HEAD_DIM, q_col=0, k_col=MOBA_HEADS,
                        v_col=2 * MOBA_HEADS, mode="moba", kmean=_block_means(proj, s1, s1),
                        name="moba_attn")
    base = 3 * MOBA_HEADS
    o_fox = _attention(proj, proj, proj, batch=batch, seq=seq, heads=FOX_HEADS,
                       dk=HEAD_DIM, dv=HEAD_DIM, q_col=base, k_col=base + FOX_HEADS,
                       v_col=base + 2 * FOX_HEADS, mode="fox", qx=qx, kx=kx,
                       name="fox_attn")
    o = jnp.concatenate([o_moba, o_fox], axis=1)
    return _matmul(o, w_out.astype(BF16), seq=seq, out_dtype=F32, epilogue="residual",
                   mod=mod, x_res=x, gate_row=2, name="l0_out_proj")


def _mla_down_kernel(x_ref, gain_ref, mod_ref, wd_ref, qn_ref, kvn_ref, cos_ref, up_ref, dn_ref,
                     cq_ref, kin_ref, hb_ref):
    def store(r0, rows, h):
        hb_ref[pl.ds(r0, rows), :] = h.astype(BF16)
    _norm_mod_rows(x_ref, gain_ref, mod_ref, 0, 1, store)
    a = jnp.dot(hb_ref[...], wd_ref[...], preferred_element_type=F32)

    def rms(v, g):
        ms = jnp.mean(v * v, axis=-1, keepdims=True)
        return (v * lax.rsqrt(ms + NORM_EPS)) * g

    q_end = MLA_Q_LORA
    kv_end = MLA_Q_LORA + MLA_KV_LORA
    cq_ref[...] = rms(a[:, :q_end], qn_ref[...]).astype(BF16)
    kin_ref[:, :MLA_KV_LORA] = rms(a[:, q_end:kv_end], kvn_ref[...]).astype(BF16)
    kpe = _rope_group(a[:, kv_end:kv_end + LANES], cos_ref[...], up_ref[...], dn_ref[...],
                      MLA_ROPE_DIM // 2)
    kin_ref[:, MLA_KV_LORA:] = kpe.astype(BF16)


def _mla_down(x, gain, mod, w_down, q_norm, kv_norm, tables, seq):
    n, d = x.shape
    tm = _pick(seq, 512)
    tiles_per_seq = seq // tm
    wcols = w_down.shape[1]
    const = lambda i: (0, 0)
    return pl.pallas_call(
        _mla_down_kernel,
        out_shape=[jax.ShapeDtypeStruct((n, MLA_Q_LORA), BF16),
                   jax.ShapeDtypeStruct((n, MLA_KV_LORA + LANES), BF16)],
        grid=(n // tm,),
        in_specs=[pl.BlockSpec((tm, d), lambda i: (i, 0)),
                  pl.BlockSpec((1, d), const),
                  pl.BlockSpec((1, 6, d), lambda i: (i // tiles_per_seq, 0, 0)),
                  pl.BlockSpec((d, wcols), const),
                  pl.BlockSpec((1, MLA_Q_LORA), const),
                  pl.BlockSpec((1, MLA_KV_LORA), const)]
                 + [pl.BlockSpec((tm, LANES), lambda i: (i % tiles_per_seq, 0))] * 3,
        out_specs=[pl.BlockSpec((tm, MLA_Q_LORA), lambda i: (i, 0)),
                   pl.BlockSpec((tm, MLA_KV_LORA + LANES), lambda i: (i, 0))],
        scratch_shapes=[pltpu.VMEM((tm, d), BF16)],
        compiler_params=_cparams(("parallel",), 40),
        name="mla_down",
    )(x, gain.reshape(1, d), mod, w_down, q_norm.reshape(1, -1), kv_norm.reshape(1, -1), *tables)


def _mla_mixer(x, gain, mod, w_dq, q_norm, w_uq, w_dkv, kv_norm, w_ukv, w_out, batch, seq):
    n, d = x.shape
    dk = 2 * LANES
    pad_rope = LANES - MLA_ROPE_DIM
    qk_scale = (MLA_NOPE_DIM + MLA_ROPE_DIM) ** -0.5 * LOG2E
    tables = _rope_tables(seq, MLA_ROPE_DIM)
    w_down = jnp.concatenate(
        [w_dq, w_dkv, jnp.zeros((d, pad_rope), F32)], axis=1).astype(BF16)
    cq, kin = _mla_down(x, gain, mod, w_down, q_norm, kv_norm, tables, seq)

    w_q = (w_uq * qk_scale).reshape(MLA_Q_LORA, MLA_HEADS, MLA_NOPE_DIM + MLA_ROPE_DIM)
    w_q = jnp.pad(w_q, ((0, 0), (0, 0), (0, pad_rope))).reshape(MLA_Q_LORA, MLA_HEADS * dk)
    tn = 512
    q = _matmul(cq, w_q.astype(BF16), seq=seq, epilogue="rot", rot_tables=tables,
                rot_groups=range(1, tn // LANES, 2), rot_half=MLA_ROPE_DIM // 2, tn=tn,
                name="mla_q_proj")

    w_kv = w_ukv.reshape(MLA_KV_LORA, MLA_HEADS, MLA_NOPE_DIM + MLA_V_DIM)
    w_knope = jnp.pad(w_kv[:, :, :MLA_NOPE_DIM], ((0, LANES), (0, 0), (0, LANES)))
    rope_place = jnp.eye(LANES, dk, k=LANES, dtype=F32) * (
        jnp.arange(LANES) < MLA_ROPE_DIM).astype(F32)[:, None]
    rope_rows = jnp.concatenate([jnp.zeros((MLA_KV_LORA, dk), F32), rope_place], axis=0)
    w_k = w_knope + rope_rows[:, None, :]
    w_v = jnp.pad(w_kv[:, :, MLA_NOPE_DIM:], ((0, LANES), (0, 0), (0, 0)))
    w_kv_all = jnp.concatenate([w_k.reshape(MLA_KV_LORA + LANES, MLA_HEADS * dk),
                                w_v.reshape(MLA_KV_LORA + LANES, MLA_HEADS * MLA_V_DIM)], axis=1)
    kv = _matmul(kin, w_kv_all.astype(BF16), seq=seq, tn=tn, name="mla_kv_proj")

    o = _attention(q, kv, kv, batch=batch, seq=seq, heads=MLA_HEADS, dk=dk, dv=MLA_V_DIM,
                   q_col=0, k_col=0, v_col=MLA_HEADS * dk // MLA_V_DIM, mode="plain",
                   name="mla_attn")
    return _matmul(o, w_out.astype(BF16), seq=seq, out_dtype=F32, epilogue="residual",
                   mod=mod, x_res=x, gate_row=2, name="l1_out_proj")


def kernel(x, c, ada_w, ada_b, norm_mix, norm_ffn, ab_w_in, ab_forget_bias, ab_w_out, mla_w_dq,
           mla_q_norm, mla_w_uq, mla_w_dkv, mla_kv_norm, mla_w_ukv, mla_w_out, router_w,
           router_bias, exp_w_gate, exp_w_up, exp_w_down, final_norm):
    batch, seq, d = x.shape
    depth = ada_w.shape[0]
    mod = _adaln(c, ada_w, ada_b)
    xs = x.reshape(batch * seq, d)
    for layer in range(depth):
        if layer % 2 == 0:
            xs = _moba_fox_mixer(xs, norm_mix[layer], mod[layer], ab_w_in, ab_forget_bias,
                                 ab_w_out, batch, seq)
        else:
            xs = _mla_mixer(xs, norm_mix[layer], mod[layer], mla_w_dq, mla_q_norm, mla_w_uq,
                            mla_w_dkv, mla_kv_norm, mla_w_ukv, mla_w_out, batch, seq)
        final_gain = final_norm if layer == depth - 1 else None
        xs = _moe_layer(xs, norm_ffn[layer], mod[layer], router_w, router_bias,
                        exp_w_gate, exp_w_up, exp_w_down, layer, seq, final_gain)
    return xs.reshape(batch, seq, d)
```

```python
import functools
import math

import jax
import jax.numpy as jnp
from jax import lax
from jax.experimental import pallas as pl
from jax.experimental.pallas import tpu as pltpu

F32 = jnp.float32
BF16 = jnp.bfloat16

HEAD_DIM = 128
MOBA_HEADS = 8
FOX_HEADS = 8
ROT_DIM = HEAD_DIM // 4
ROPE_THETA = 500000.0
MOBA_BLOCK = 256
MOBA_TOPK = 3
MLA_HEADS = 16
MLA_Q_LORA = 512
MLA_KV_LORA = 512
MLA_NOPE_DIM = 128
MLA_ROPE_DIM = 64
MLA_V_DIM = 128
N_EXPERTS = 16
N_GROUPS = 4
EXPERTS_PER_GROUP = N_EXPERTS // N_GROUPS
TOP_K = 2
NORM_EPS = 1e-6

LANES = 128
V7X_VMEM_BYTES = 64 * 1024 * 1024
MIB = 1024 * 1024
LOG2E = math.log2(math.e)

NT_DIMS = (((1,), (1,)), ((), ()))


def _cparams(semantics, vmem_mib):
    assert vmem_mib * MIB < V7X_VMEM_BYTES
    return pltpu.CompilerParams(dimension_semantics=semantics,
                                vmem_limit_bytes=vmem_mib * MIB)


def _pick(n, pref):
    t = min(pref, n)
    while n % t:
        t //= 2
    return t


def _adaln_kernel(c_ref, w_ref, b_ref, o_ref):
    c = c_ref[...]
    ca = (c * jax.nn.sigmoid(c)).astype(BF16)
    acc = jnp.dot(ca, w_ref[0].astype(BF16), preferred_element_type=F32)
    o_ref[0] = acc + b_ref[0]


def _adaln(c, ada_w, ada_b):
    depth, d, n6 = ada_w.shape
    b = c.shape[0]
    rows = 8
    c_pad = jnp.pad(c, ((0, rows - b), (0, 0)))
    tn = _pick(n6, 1024)
    out = pl.pallas_call(
        _adaln_kernel,
        out_shape=jax.ShapeDtypeStruct((depth, rows, n6), F32),
        grid=(depth, n6 // tn),
        in_specs=[
            pl.BlockSpec((rows, d), lambda l, j: (0, 0)),
            pl.BlockSpec((1, d, tn), lambda l, j: (l, 0, j)),
            pl.BlockSpec((1, 1, tn), lambda l, j: (l, 0, j)),
        ],
        out_specs=pl.BlockSpec((1, rows, tn), lambda l, j: (l, 0, j)),
        compiler_params=_cparams(("parallel", "parallel"), 40),
        name="adaln_mod",
    )(c_pad, ada_w, ada_b.reshape(depth, 1, n6))
    return out[:, :b, :].reshape(depth, b, 6, d)


NORM_ROWS = 256


def _norm_mod_rows(x_ref, gain_ref, mod_ref, shift_row, scale_row, store):
    tm = x_ref.shape[0]
    rows = min(NORM_ROWS, tm)
    gain = gain_ref[...]
    scale1p = 1.0 + mod_ref[0, scale_row:scale_row + 1, :]
    shift = mod_ref[0, shift_row:shift_row + 1, :]

    def body(c, carry):
        r0 = pl.multiple_of(c * rows, rows)
        x = x_ref[pl.ds(r0, rows), :]
        ms = jnp.mean(x * x, axis=-1, keepdims=True)
        y = x * lax.rsqrt(ms + NORM_EPS)
        h = (y * gain) * scale1p + shift
        store(r0, rows, h)
        return carry

    lax.fori_loop(0, tm // rows, body, 0)


def _rope_group(a, cos_t, sin_up, sin_dn, half):
    up = pltpu.roll(a, LANES - half, 1)
    dn = pltpu.roll(a, half, 1)
    return a * cos_t + up * sin_up + dn * sin_dn


def _rope_tables(seq, rot_dim):
    half = rot_dim // 2
    inv_freq = ROPE_THETA ** (-jnp.arange(half, dtype=F32) / half)
    ang = jnp.arange(seq, dtype=F32)[:, None] * inv_freq[None, :]
    cos, sin = jnp.cos(ang), jnp.sin(ang)
    zeros = jnp.zeros((seq, LANES - rot_dim), F32)
    zh = jnp.zeros((seq, half), F32)
    cos_t = jnp.concatenate([cos, cos, jnp.ones((seq, LANES - rot_dim), F32)], axis=1)
    sin_up = jnp.concatenate([-sin, zh, zeros], axis=1)
    sin_dn = jnp.concatenate([zh, sin, zeros], axis=1)
    return cos_t, sin_up, sin_dn


def _mm_kernel(*refs, prologue, epilogue, rot_groups, rot_tile_limit, rot_half,
               shift_row, scale_row, gate_row, has_side):
    it = iter(refs)
    a_ref = next(it)
    if prologue == "norm":
        gain_ref = next(it)
    if prologue == "norm" or epilogue == "residual":
        mod_ref = next(it)
    w_ref = next(it)
    if has_side:
        wside_ref = next(it)
    if epilogue == "rot":
        cos_ref, up_ref, dn_ref = next(it), next(it), next(it)
    if epilogue == "residual":
        x_ref = next(it)
    o_ref = next(it)
    if has_side:
        side_ref = next(it)
    if prologue == "norm":
        hb_ref = next(it)

    j = pl.program_id(1)

    if prologue == "norm":
        @pl.when(j == 0)
        def _():
            def store(r0, rows, h):
                hb_ref[pl.ds(r0, rows), :] = h.astype(BF16)
            _norm_mod_rows(a_ref, gain_ref, mod_ref, shift_row, scale_row, store)
            if has_side:
                side_ref[...] = jnp.dot(hb_ref[...], wside_ref[...],
                                        preferred_element_type=F32)
        lhs_ref = hb_ref
    else:
        lhs_ref = a_ref

    def product():
        return jnp.dot(lhs_ref[...], w_ref[...], preferred_element_type=F32)

    if epilogue == "rot":
        def rotated():
            acc = product()
            groups = []
            for g in range(acc.shape[1] // LANES):
                blk = acc[:, g * LANES:(g + 1) * LANES]
                if g in rot_groups:
                    blk = _rope_group(blk, cos_ref[...], up_ref[...], dn_ref[...], rot_half)
                groups.append(blk)
            return jnp.concatenate(groups, axis=1)

        if rot_tile_limit is None:
            o_ref[...] = rotated().astype(o_ref.dtype)
        else:
            @pl.when(j < rot_tile_limit)
            def _():
                o_ref[...] = rotated().astype(o_ref.dtype)

            @pl.when(j >= rot_tile_limit)
            def _():
                o_ref[...] = product().astype(o_ref.dtype)
    elif epilogue == "residual":
        gate = mod_ref[0, gate_row:gate_row + 1, :]
        o_ref[...] = x_ref[...] + gate * product()
    else:
        o_ref[...] = product().astype(o_ref.dtype)


def _matmul(a, w, *, seq, out_dtype=BF16, tm=1024, tn=512, prologue="none", gain=None,
            mod=None, shift_row=0, scale_row=1, epilogue="none", rot_tables=None,
            rot_groups=(), rot_tile_limit=None, rot_half=0, x_res=None, gate_row=2,
            w_side=None, name="matmul"):
    m, k = a.shape
    n = w.shape[1]
    tm = _pick(seq, tm)
    tn = _pick(n, tn)
    tiles_per_seq = seq // tm
    has_side = w_side is not None
    in_specs = [pl.BlockSpec((tm, k), lambda i, j: (i, 0))]
    args = [a]
    if prologue == "norm":
        in_specs.append(pl.BlockSpec((1, k), lambda i, j: (0, 0)))
        args.append(gain.reshape(1, k))
    if prologue == "norm":
        in_specs.append(pl.BlockSpec((1, 6, k), lambda i, j: (i // tiles_per_seq, 0, 0)))
        args.append(mod)
    elif epilogue == "residual":
        in_specs.append(pl.BlockSpec((1, 6, tn), lambda i, j: (i // tiles_per_seq, 0, j)))
        args.append(mod)
    in_specs.append(pl.BlockSpec((k, tn), lambda i, j: (0, j)))
    args.append(w)
    if has_side:
        in_specs.append(pl.BlockSpec((k, LANES), lambda i, j: (0, 0)))
        args.append(w_side)
    if epilogue == "rot":
        for t in rot_tables:
            in_specs.append(pl.BlockSpec((tm, LANES), lambda i, j: (i % tiles_per_seq, 0)))
            args.append(t)
    if epilogue == "residual":
        in_specs.append(pl.BlockSpec((tm, tn), lambda i, j: (i, j)))
        args.append(x_res)
    out_shape = [jax.ShapeDtypeStruct((m, n), out_dtype)]
    out_specs = [pl.BlockSpec((tm, tn), lambda i, j: (i, j))]
    if has_side:
        out_shape.append(jax.ShapeDtypeStruct((m, LANES), F32))
        out_specs.append(pl.BlockSpec((tm, LANES), lambda i, j: (i, 0)))
    scratch = [pltpu.VMEM((tm, k), BF16)] if prologue == "norm" else []
    kernel = functools.partial(
        _mm_kernel, prologue=prologue, epilogue=epilogue, rot_groups=tuple(rot_groups),
        rot_tile_limit=rot_tile_limit, rot_half=rot_half, shift_row=shift_row,
        scale_row=scale_row, gate_row=gate_row, has_side=has_side)
    outs = pl.pallas_call(
        kernel,
        out_shape=out_shape,
        grid=(m // tm, n // tn),
        in_specs=in_specs,
        out_specs=out_specs,
        scratch_shapes=scratch,
        compiler_params=_cparams(("parallel", "arbitrary"), 48),
        name=name,
    )(*args)
    return outs if has_side else outs[0]


def _fox_prep_kernel(flog_ref, bias_ref, qx_ref, kx_ref, carry_ref):
    tc = flog_ref.shape[0]

    @pl.when(pl.program_id(1) == 0)
    def _():
        carry_ref[...] = jnp.zeros_like(carry_ref)

    z = flog_ref[...] + bias_ref[...]
    logf = jnp.minimum(z, 0.0) - jnp.log1p(jnp.exp(-jnp.abs(z)))
    row = lax.broadcasted_iota(jnp.int32, (tc, tc), 0)
    col = lax.broadcasted_iota(jnp.int32, (tc, tc), 1)
    tri = jnp.where(col <= row, 1.0, 0.0).astype(F32)
    csum = jnp.dot(tri, logf, preferred_element_type=F32,
                   precision=lax.Precision.HIGHEST) + carry_ref[...]
    carry_ref[...] = csum[tc - 1:tc, :]
    csum = csum * LOG2E
    lane = lax.broadcasted_iota(jnp.int32, (tc, LANES), 1)
    one = jnp.ones((tc, LANES), F32)
    zero = jnp.zeros((tc, LANES), F32)
    for h in range(FOX_HEADS):
        colv = jnp.broadcast_to(csum[:, h:h + 1], (tc, LANES))
        hi = colv.astype(BF16).astype(F32)
        r1 = colv - hi
        mid = r1.astype(BF16).astype(F32)
        lo = r1 - mid
        qv = jnp.where(lane == 0, hi, jnp.where(lane == 1, mid, jnp.where(
            lane == 2, lo, jnp.where(lane < 6, one, zero))))
        kv = jnp.where(lane < 3, one, jnp.where(lane == 3, -hi, jnp.where(
            lane == 4, -mid, jnp.where(lane == 5, -lo, zero))))
        qx_ref[:, h * LANES:(h + 1) * LANES] = qv.astype(BF16)
        kx_ref[:, h * LANES:(h + 1) * LANES] = kv.astype(BF16)


def _fox_prep(flog, forget_bias, batch, seq):
    n = flog.shape[0]
    tc = _pick(seq, 256)
    bias = jnp.pad(forget_bias, (0, LANES - FOX_HEADS)).reshape(1, LANES)
    width = FOX_HEADS * LANES
    steps = seq // tc
    return pl.pallas_call(
        _fox_prep_kernel,
        out_shape=[jax.ShapeDtypeStruct((n, width), BF16)] * 2,
        grid=(batch, steps),
        in_specs=[pl.BlockSpec((tc, LANES), lambda b, s: (b * steps + s, 0)),
                  pl.BlockSpec((1, LANES), lambda b, s: (0, 0))],
        out_specs=[pl.BlockSpec((tc, width), lambda b, s: (b * steps + s, 0))] * 2,
        scratch_shapes=[pltpu.VMEM((1, LANES), F32)],
        compiler_params=_cparams(("parallel", "arbitrary"), 32),
        name="fox_prep",
    )(flog, bias)


ATTN_TILE = 512
ATTN_HEADS_PER_STEP = 4


M_INIT = -1e30


def _attn_kernel(qt_ref, kt_ref, *refs, mode, tq, dk, dv):
    it = iter(refs)
    q_ref, k_ref, v_ref = next(it), next(it), next(it)
    if mode == "fox":
        qx_ref, kx_ref = next(it), next(it)
    if mode == "moba":
        kmean_ref = next(it)
    o_ref = next(it)
    m_ref, l_ref, acc_ref = next(it), next(it), next(it)
    if mode == "moba":
        sel_ref = next(it)

    p_id = pl.program_id(2)
    i, j = qt_ref[p_id], kt_ref[p_id]
    neg_inf = jnp.float32(-jnp.inf)
    heads = q_ref.shape[1] // dk
    blocks_per_tile = tq // MOBA_BLOCK
    block_shift = MOBA_BLOCK.bit_length() - 1

    def q_of(h):
        q = q_ref[:, h * dk:(h + 1) * dk]
        if mode == "fox":
            q = jnp.concatenate([q, qx_ref[:, h * LANES:(h + 1) * LANES]], axis=1)
        return q

    def scores(h):
        k = k_ref[:, h * dk:(h + 1) * dk]
        if mode == "fox":
            k = jnp.concatenate([k, kx_ref[:, h * LANES:(h + 1) * LANES]], axis=1)
        return lax.dot_general(q_of(h), k, NT_DIMS, preferred_element_type=F32)

    def block_hits(h):
        lane = lax.broadcasted_iota(jnp.int32, (tq, LANES), 1)
        sel = sel_ref[h]
        return [jnp.broadcast_to(
            jnp.max(jnp.where(lane == j * blocks_per_tile + c, sel, 0.0), axis=1, keepdims=True),
            (tq, LANES)) > 0.0 for c in range(blocks_per_tile)]

    def masked(s, allowed_of_group):
        groups = [jnp.where(allowed_of_group(g), s[:, g * LANES:(g + 1) * LANES], neg_inf)
                  for g in range(tq // LANES)]
        return jnp.concatenate(groups, axis=1)

    def update(h, s):
        m_old = m_ref[h]
        row_max = jnp.broadcast_to(jnp.max(s, axis=1, keepdims=True), m_old.shape)
        m_new = jnp.maximum(m_old, row_max)
        alpha = jnp.exp2(m_old - m_new)
        p = jnp.exp2(s - jnp.concatenate([m_new] * (s.shape[1] // LANES), axis=1))
        row_sum = jnp.broadcast_to(jnp.sum(p, axis=1, keepdims=True), m_old.shape)
        l_ref[h] = alpha * l_ref[h] + row_sum
        acc_ref[h] = alpha * acc_ref[h] + jnp.dot(
            p.astype(BF16), v_ref[:, h * dv:(h + 1) * dv], preferred_element_type=F32)
        m_ref[h] = m_new

    @pl.when(j == 0)
    def _():
        m_ref[...] = jnp.full_like(m_ref, M_INIT)
        l_ref[...] = jnp.zeros_like(l_ref)
        acc_ref[...] = jnp.zeros_like(acc_ref)
        if mode == "moba":
            nb = kmean_ref.shape[0]
            lane = lax.broadcasted_iota(jnp.int32, (tq, LANES), 1)
            lane_f = lane.astype(F32)
            row_block = jnp.right_shift(lax.broadcasted_iota(jnp.int32, (tq, LANES), 0),
                                        block_shift)
            past = lane < i * blocks_per_tile + row_block
            for h in range(heads):
                kmean = jnp.concatenate(
                    [kmean_ref[:, h * dk:(h + 1) * dk], jnp.zeros((LANES - nb, dk), F32)], axis=0)
                gate = lax.dot_general(q_of(h).astype(F32), kmean, NT_DIMS,
                                       preferred_element_type=F32,
                                       precision=lax.Precision.HIGHEST)
                g = jnp.where(past, gate, neg_inf)
                chosen = jnp.zeros(gate.shape, F32)
                for _ in range(min(MOBA_TOPK, nb - 1)):
                    mx = jnp.broadcast_to(jnp.max(g, axis=1, keepdims=True), g.shape)
                    first = jnp.broadcast_to(
                        jnp.min(jnp.where(g == mx, lane_f, float(LANES)), axis=1, keepdims=True),
                        g.shape)
                    pick = lane_f == first
                    chosen = jnp.where(jnp.logical_and(pick, past), 1.0, chosen)
                    g = jnp.where(pick, neg_inf, g)
                sel_ref[h] = chosen

    groups_per_block = MOBA_BLOCK // LANES

    @pl.when(j < i)
    def _():
        for h in range(heads):
            s = scores(h)
            if mode == "moba":
                hits = block_hits(h)
                s = masked(s, lambda g: hits[g // groups_per_block])
            update(h, s)

    @pl.when(j == i)
    def _():
        row = lax.broadcasted_iota(jnp.int32, (tq, LANES), 0)
        lane = lax.broadcasted_iota(jnp.int32, (tq, LANES), 1)
        row_blk = jnp.right_shift(row, block_shift)
        for h in range(heads):
            s = scores(h)
            if mode == "moba":
                hits = block_hits(h)

                def allowed(g):
                    col_blk = g // groups_per_block
                    own = jnp.logical_and(row_blk == col_blk, lane + g * LANES <= row)
                    return jnp.logical_or(own, jnp.logical_and(row_blk > col_blk, hits[col_blk]))
            else:
                def allowed(g):
                    return lane + g * LANES <= row
            update(h, masked(s, allowed))
            o_ref[:, h * dv:(h + 1) * dv] = (acc_ref[h] / l_ref[h]).astype(o_ref.dtype)


def _attention(q_arr, k_arr, v_arr, *, batch, seq, heads, dk, dv, q_col, k_col, v_col, mode,
               qx=None, kx=None, kmean=None, name="attn"):
    n = batch * seq
    tq = _pick(seq, ATTN_TILE)
    g = ATTN_HEADS_PER_STEP
    assert tq % MOBA_BLOCK == 0 and heads % g == 0
    assert q_col % g == 0 and k_col % g == 0 and v_col % g == 0
    nq = seq // tq
    pairs = [(i, j) for i in range(nq) for j in range(i + 1)]
    q_tile = jnp.array([p[0] for p in pairs], jnp.int32)
    k_tile = jnp.array([p[1] for p in pairs], jnp.int32)

    def q_rows(b, h, p, qt, kt):
        return b * nq + qt[p]

    def k_rows(b, h, p, qt, kt):
        return b * nq + kt[p]

    in_specs = [
        pl.BlockSpec((tq, g * dk), lambda b, h, p, qt, kt: (q_rows(b, h, p, qt, kt), q_col // g + h)),
        pl.BlockSpec((tq, g * dk), lambda b, h, p, qt, kt: (k_rows(b, h, p, qt, kt), k_col // g + h)),
        pl.BlockSpec((tq, g * dv), lambda b, h, p, qt, kt: (k_rows(b, h, p, qt, kt), v_col // g + h)),
    ]
    args = [q_arr, k_arr, v_arr]
    if mode == "fox":
        in_specs += [
            pl.BlockSpec((tq, g * LANES), lambda b, h, p, qt, kt: (q_rows(b, h, p, qt, kt), h)),
            pl.BlockSpec((tq, g * LANES), lambda b, h, p, qt, kt: (k_rows(b, h, p, qt, kt), h))]
        args += [qx, kx]
    if mode == "moba":
        nb = seq // MOBA_BLOCK
        in_specs.append(pl.BlockSpec((nb, g * dk), lambda b, h, p, qt, kt: (b, h)))
        args.append(kmean)
    assert dv == LANES
    scratch = [pltpu.VMEM((g, tq, LANES), F32), pltpu.VMEM((g, tq, LANES), F32),
               pltpu.VMEM((g, tq, dv), F32)]
    if mode == "moba":
        scratch.append(pltpu.VMEM((g, tq, LANES), F32))
    return pl.pallas_call(
        functools.partial(_attn_kernel, mode=mode, tq=tq, dk=dk, dv=dv),
        out_shape=jax.ShapeDtypeStruct((n, heads * dv), BF16),
        grid_spec=pltpu.PrefetchScalarGridSpec(
            num_scalar_prefetch=2,
            grid=(batch, heads // g, len(pairs)),
            in_specs=in_specs,
            out_specs=pl.BlockSpec((tq, g * dv),
                                   lambda b, h, p, qt, kt: (q_rows(b, h, p, qt, kt), h)),
            scratch_shapes=scratch,
        ),
        compiler_params=_cparams(("parallel", "parallel", "arbitrary"), 40),
        name=name,
    )(q_tile, k_tile, *args)


def _block_means_kernel(k_ref, o_ref):
    for b in range(o_ref.shape[0]):
        kb = k_ref[b * MOBA_BLOCK:(b + 1) * MOBA_BLOCK, :].astype(F32)
        o_ref[b:b + 1, :] = jnp.mean(kb, axis=0, keepdims=True)


def _block_means(arr, col0, cols):
    n = arr.shape[0]
    blocks = 8
    rows = blocks * MOBA_BLOCK
    assert n % rows == 0 and col0 % cols == 0
    return pl.pallas_call(
        _block_means_kernel,
        out_shape=jax.ShapeDtypeStruct((n // MOBA_BLOCK, cols), F32),
        grid=(n // rows,),
        in_specs=[pl.BlockSpec((rows, cols), lambda i: (i, col0 // cols))],
        out_specs=pl.BlockSpec((blocks, cols), lambda i: (i, 0)),
        compiler_params=_cparams(("parallel",), 32),
        name="moba_block_means",
    )(arr)


def _top2_of4(vals):
    a, b, c, d = vals
    hi1, lo1 = jnp.maximum(a, b), jnp.minimum(a, b)
    hi2, lo2 = jnp.maximum(c, d), jnp.minimum(c, d)
    return jnp.maximum(hi1, hi2) + jnp.maximum(jnp.minimum(hi1, hi2), jnp.maximum(lo1, lo2))


def _route_kernel(x_ref, gain_ref, mod_ref, rwt_ref, rbias_ref, h_ref, idx_ref, wgt_ref):
    def store(r0, rows, h):
        h_ref[pl.ds(r0, rows), :] = h
    _norm_mod_rows(x_ref, gain_ref, mod_ref, 3, 4, store)

    logits = lax.dot_general(rwt_ref[...], h_ref[...], NT_DIMS, preferred_element_type=F32,
                             precision=lax.Precision.HIGHEST)
    score = jax.nn.sigmoid(logits)
    biased = score + rbias_ref[...]
    rows_b = [biased[e:e + 1, :] for e in range(N_EXPERTS)]
    rows_s = [score[e:e + 1, :] for e in range(N_EXPERTS)]
    grp = [_top2_of4(rows_b[g * EXPERTS_PER_GROUP:(g + 1) * EXPERTS_PER_GROUP])
           for g in range(N_GROUPS)]
    best, g_sel = grp[0], jnp.zeros_like(grp[0], dtype=jnp.int32)
    for g in range(1, N_GROUPS):
        better = grp[g] > best
        best = jnp.where(better, grp[g], best)
        g_sel = jnp.where(better, g, g_sel)

    def in_group(rows, r):
        out = rows[r]
        for g in range(1, N_GROUPS):
            out = jnp.where(g_sel == g, rows[g * EXPERTS_PER_GROUP + r], out)
        return out

    cb = [in_group(rows_b, r) for r in range(EXPERTS_PER_GROUP)]
    cs = [in_group(rows_s, r) for r in range(EXPERTS_PER_GROUP)]
    neg_inf = jnp.float32(-jnp.inf)
    picks = []
    for _ in range(TOP_K):
        bv, bi, bs = cb[0], jnp.zeros_like(g_sel), cs[0]
        for r in range(1, EXPERTS_PER_GROUP):
            better = cb[r] > bv
            bv = jnp.where(better, cb[r], bv)
            bi = jnp.where(better, r, bi)
            bs = jnp.where(better, cs[r], bs)
        picks.append((bi, bs))
        cb = [jnp.where(bi == r, neg_inf, cb[r]) for r in range(EXPERTS_PER_GROUP)]
    (i0, s0), (i1, s1) = picks
    total = s0 + s1
    idx_ref[0:1, :] = g_sel * EXPERTS_PER_GROUP + i0
    idx_ref[1:2, :] = g_sel * EXPERTS_PER_GROUP + i1
    wgt_ref[0:1, :] = s0 / total
    wgt_ref[1:2, :] = s1 / total


def _route(x, gain, mod, router_w, router_bias, seq):
    n, d = x.shape
    tm = _pick(seq, 512)
    tiles_per_seq = seq // tm
    return pl.pallas_call(
        _route_kernel,
        out_shape=[jax.ShapeDtypeStruct((n, d), F32),
                   jax.ShapeDtypeStruct((TOP_K, n), jnp.int32),
                   jax.ShapeDtypeStruct((TOP_K, n), F32)],
        grid=(n // tm,),
        in_specs=[pl.BlockSpec((tm, d), lambda i: (i, 0)),
                  pl.BlockSpec((1, d), lambda i: (0, 0)),
                  pl.BlockSpec((1, 6, d), lambda i: (i // tiles_per_seq, 0, 0)),
                  pl.BlockSpec((N_EXPERTS, d), lambda i: (0, 0)),
                  pl.BlockSpec((N_EXPERTS, 1), lambda i: (0, 0))],
        out_specs=[pl.BlockSpec((tm, d), lambda i: (i, 0)),
                   pl.BlockSpec((TOP_K, tm), lambda i: (0, i)),
                   pl.BlockSpec((TOP_K, tm), lambda i: (0, i))],
        compiler_params=_cparams(("parallel",), 32),
        name="moe_route",
    )(x, gain.reshape(1, d), mod, router_w.T, router_bias.reshape(N_EXPERTS, 1))


GMM_ROWS = 512
GMM_UP_COLS = 128
GMM_DOWN_COLS = 512
GMM_ISSUE_STEPS = 8
PACK_ROWS = 256
DMA_UNROLL = 8


def _pack_kernel(wg_ref, wu_ref, wd_ref, wgu_ref, wdb_ref):
    n_up = wgu_ref.shape[1]
    for f in range(n_up):
        cols = slice(f * GMM_UP_COLS, (f + 1) * GMM_UP_COLS)
        wgu_ref[0, f, :, :GMM_UP_COLS] = wg_ref[0, 0, :, cols].astype(BF16)
        wgu_ref[0, f, :, GMM_UP_COLS:] = wu_ref[0, 0, :, cols].astype(BF16)
    wdb_ref[0] = wd_ref[0, 0].astype(BF16)


def _pack_expert_weights(w_gate, w_up, w_down, layer):
    _, n_exp, d, d_exp = w_gate.shape
    n_up = d_exp // GMM_UP_COLS
    steps = d // PACK_ROWS
    down_rows = d_exp // steps
    assert d_exp % GMM_UP_COLS == 0 and d % PACK_ROWS == 0 and down_rows % 16 == 0
    up_spec = pl.BlockSpec((1, 1, PACK_ROWS, d_exp), lambda e, k: (layer, e, k, 0))
    return pl.pallas_call(
        _pack_kernel,
        out_shape=[jax.ShapeDtypeStruct((n_exp, n_up, d, 2 * GMM_UP_COLS), BF16),
                   jax.ShapeDtypeStruct((n_exp, d_exp, d), BF16)],
        grid=(n_exp, steps),
        in_specs=[up_spec, up_spec,
                  pl.BlockSpec((1, 1, down_rows, d), lambda e, k: (layer, e, k, 0))],
        out_specs=[pl.BlockSpec((1, n_up, PACK_ROWS, 2 * GMM_UP_COLS), lambda e, k: (e, 0, k, 0)),
                   pl.BlockSpec((1, down_rows, d), lambda e, k: (e, k, 0))],
        compiler_params=_cparams(("parallel", "parallel"), 32),
        name="moe_pack_weights",
    )(w_gate, w_up, w_down)


def _ffn_kernel(eid_ref, nvalid_ref, dst_ref, h_hbm, wgu_ref, wd_ref, y_hbm,
                xbuf_ref, hb_ref, act_ref, ystage_ref, gsem, ssem):
    t = pl.program_id(0)
    rows = hb_ref.shape[0]
    n_up = act_ref.shape[0]
    n_tokens = h_hbm.shape[0]
    per_chunk = rows // GMM_ISSUE_STEPS
    nv = nvalid_ref[0]
    valid = t < nv

    def dst_of(tile, r):
        return dst_ref[(tile + 1) * rows + r]

    def token_of(dst_row):
        if n_tokens & (n_tokens - 1) == 0:
            return jnp.bitwise_and(dst_row, n_tokens - 1)
        return lax.rem(dst_row, n_tokens)

    def gather_row(r, src_row):
        return pltpu.make_async_copy(h_hbm.at[pl.ds(src_row, 1), :],
                                     xbuf_ref.at[pl.ds(r, 1), :], gsem)

    def scatter_row(r, dst_row):
        return pltpu.make_async_copy(ystage_ref.at[pl.ds(r, 1), :],
                                     y_hbm.at[pl.ds(dst_row, 1), :], ssem)

    def for_rows(fn):
        def body(r, carry):
            fn(r)
            return carry
        lax.fori_loop(0, rows, body, 0, unroll=DMA_UNROLL)

    @pl.when(t == 0)
    def _():
        ystage_ref[...] = jnp.zeros_like(ystage_ref)
        for_rows(lambda r: gather_row(r, token_of(dst_of(0, r))).start())

    @pl.when(t <= nv)
    def _():
        for_rows(lambda r: gather_row(r, 0).wait())

    @pl.when(t == nv)
    def _():
        for_rows(lambda r: scatter_row(r, dst_of(t - 1, r)).start())
        for_rows(lambda r: scatter_row(r, 0).wait())

    @pl.when(valid)
    def _():
        hb_ref[...] = xbuf_ref[...].astype(BF16)
        for f in range(n_up):
            if f < GMM_ISSUE_STEPS:
                for r in range(f * per_chunk, (f + 1) * per_chunk):
                    gather_row(r, token_of(dst_of(t + 1, r))).start()
                    scatter_row(r, dst_of(t - 1, r)).start()
            gu = jnp.dot(hb_ref[...], wgu_ref[0, f], preferred_element_type=F32)
            g, u = gu[:, :GMM_UP_COLS], gu[:, GMM_UP_COLS:]
            act_ref[f] = ((g * jax.nn.sigmoid(g)) * u).astype(BF16)
        for_rows(lambda r: scatter_row(r, 0).wait())
        act = jnp.concatenate([act_ref[k] for k in range(n_up)], axis=1)
        ystage_ref[...] = jnp.dot(act, wd_ref[0], preferred_element_type=F32)


def _expert_ffn(eid, nvalid, dst, h, wgu, wdb, n_slots):
    n_tiles = dst.shape[0] // GMM_ROWS - 1
    d = h.shape[1]
    n_exp, n_up, _, up_cols = wgu.shape
    d_exp = wdb.shape[1]
    assert GMM_ROWS % GMM_ISSUE_STEPS == 0 and n_up >= GMM_ISSUE_STEPS
    return pl.pallas_call(
        _ffn_kernel,
        out_shape=jax.ShapeDtypeStruct((n_slots + GMM_ROWS, d), F32),
        grid_spec=pltpu.PrefetchScalarGridSpec(
            num_scalar_prefetch=3,
            grid=(n_tiles,),
            in_specs=[
                pl.BlockSpec(memory_space=pl.ANY),
                pl.BlockSpec((1, n_up, d, up_cols), lambda t, e, nv, ds: (e[t], 0, 0, 0)),
                pl.BlockSpec((1, d_exp, d), lambda t, e, nv, ds: (e[t], 0, 0)),
            ],
            out_specs=pl.BlockSpec(memory_space=pl.ANY),
            scratch_shapes=[pltpu.VMEM((GMM_ROWS, d), F32),
                            pltpu.VMEM((GMM_ROWS, d), BF16),
                            pltpu.VMEM((n_up, GMM_ROWS, GMM_UP_COLS), BF16),
                            pltpu.VMEM((GMM_ROWS, d), F32),
                            pltpu.SemaphoreType.DMA,
                            pltpu.SemaphoreType.DMA],
        ),
        compiler_params=_cparams(("arbitrary",), 56),
        name="moe_ffn",
    )(eid, nvalid, dst, h, wgu, wdb)


def _combine_kernel(x_ref, y0_ref, y1_ref, w_ref, mod_ref, *rest, final):
    if final:
        gain_ref, o_ref = rest
    else:
        (o_ref,) = rest
    rows = min(NORM_ROWS, x_ref.shape[0])
    gate = mod_ref[0, 5:6, :]

    def body(c, carry):
        r0 = pl.multiple_of(c * rows, rows)
        w = w_ref[pl.ds(r0, rows), :]
        y = (w[:, 0:1] * y0_ref[pl.ds(r0, rows), :]
             + w[:, 1:2] * y1_ref[pl.ds(r0, rows), :])
        out = x_ref[pl.ds(r0, rows), :] + gate * y
        if final:
            ms = jnp.mean(out * out, axis=-1, keepdims=True)
            out = (out * lax.rsqrt(ms + NORM_EPS)) * gain_ref[...]
        o_ref[pl.ds(r0, rows), :] = out
        return carry

    lax.fori_loop(0, x_ref.shape[0] // rows, body, 0)


def _combine(x, y_slots, wgt, mod, seq, final_gain=None):
    n, d = x.shape
    tm = _pick(seq, 512)
    tiles_per_seq = seq // tm
    tiles = n // tm
    final = final_gain is not None
    in_specs = [pl.BlockSpec((tm, d), lambda i: (i, 0)),
                pl.BlockSpec((tm, d), lambda i: (i, 0)),
                pl.BlockSpec((tm, d), lambda i: (tiles + i, 0)),
                pl.BlockSpec((tm, TOP_K), lambda i: (i, 0)),
                pl.BlockSpec((1, 6, d), lambda i: (i // tiles_per_seq, 0, 0))]
    args = [x, y_slots, y_slots, wgt, mod]
    if final:
        in_specs.append(pl.BlockSpec((1, d), lambda i: (0, 0)))
        args.append(final_gain.reshape(1, d))
    return pl.pallas_call(
        functools.partial(_combine_kernel, final=final),
        out_shape=jax.ShapeDtypeStruct((n, d), F32),
        grid=(tiles,),
        in_specs=in_specs,
        out_specs=pl.BlockSpec((tm, d), lambda i: (i, 0)),
        compiler_params=_cparams(("parallel",), 48),
        name="moe_combine",
    )(*args)


def _dispatch_plan(idx):
    n = idx.shape[1]
    slots = TOP_K * n
    cap = slots + N_EXPERTS * GMM_ROWS
    expert = idx.reshape(slots)
    onehot = (expert[:, None] == jnp.arange(N_EXPERTS)[None, :]).astype(jnp.int32)
    csum = jnp.cumsum(onehot, axis=0)
    rank = jnp.sum(csum * onehot, axis=1) - 1
    counts = csum[-1]
    padded = -(-counts // GMM_ROWS) * GMM_ROWS
    ends = jnp.cumsum(padded)
    pos = (ends - padded)[expert] + rank
    slot_ids = jnp.arange(slots, dtype=jnp.int32)
    dummy = slots + jnp.arange(cap + GMM_ROWS, dtype=jnp.int32) % GMM_ROWS
    dst = dummy.at[pos + GMM_ROWS].set(slot_ids)
    nvalid = (ends[-1] // GMM_ROWS).astype(jnp.int32)
    tile_start = jnp.arange(cap // GMM_ROWS, dtype=jnp.int32) * GMM_ROWS
    tile_start = jnp.minimum(tile_start, ends[-1] - GMM_ROWS)
    eid = jnp.sum(tile_start[:, None] >= ends[None, :], axis=1).astype(jnp.int32)
    return dst, eid, nvalid.reshape(1)


def _moe_layer(x, gain, mod, router_w, router_bias, w_gate, w_up, w_down, layer, seq,
               final_gain):
    wgu, wdb = _pack_expert_weights(w_gate, w_up, w_down, layer)
    h, idx, wgt = _route(x, gain, mod, router_w, router_bias, seq)
    dst, eid, nvalid = _dispatch_plan(idx)
    y_slots = _expert_ffn(eid, nvalid, dst, h, wgu, wdb, TOP_K * x.shape[0])
    return _combine(x, y_slots, wgt.T, mod, seq, final_gain)


def _moba_fox_mixer(x, gain, mod, w_in, forget_bias, w_out, batch, seq):
    n, d = x.shape
    mix = (MOBA_HEADS + FOX_HEADS) * HEAD_DIM
    qk_scale = HEAD_DIM ** -0.5 * LOG2E
    s1 = MOBA_HEADS * HEAD_DIM
    s2 = FOX_HEADS * HEAD_DIM
    col_scale = jnp.ones((3 * mix,), F32)
    col_scale = col_scale.at[:s1].set(qk_scale).at[3 * s1:3 * s1 + s2].set(qk_scale)
    w_main = (w_in[:, :3 * mix] * col_scale[None, :]).astype(BF16)
    w_forget = jnp.pad(w_in[:, 3 * mix:], ((0, 0), (0, LANES - FOX_HEADS))).astype(BF16)
    tn = 512
    proj, flog = _matmul(
        x, w_main, seq=seq, prologue="norm", gain=gain, mod=mod, shift_row=0, scale_row=1,
        epilogue="rot", rot_tables=_rope_tables(seq, ROT_DIM), rot_groups=range(tn // LANES),
        rot_tile_limit=2 * s1 // tn, rot_half=ROT_DIM // 2, w_side=w_forget, tn=tn,
        name="l0_in_proj")
    qx, kx = _fox_prep(flog, forget_bias, batch, seq)
    o_moba = _attention(proj, proj, proj, batch=batch, seq=seq, heads=MOBA_HEADS,
                        dk=HEAD_DIM, dv=HEAD_DIM, q_col=0, k_col=MOBA_HEADS,
                        v_col=2 * MOBA_HEADS, mode="moba", kmean=_block_means(proj, s1, s1),
                        name="moba_attn")
    base = 3 * MOBA_HEADS
    o_fox = _attention(proj, proj, proj, batch=batch, seq=seq, heads=FOX_HEADS,
                       dk=HEAD_DIM, dv=HEAD_DIM, q_col=base, k_col=base + FOX_HEADS,
                       v_col=base + 2 * FOX_HEADS, mode="fox", qx=qx, kx=kx,
                       name="fox_attn")
    o = jnp.concatenate([o_moba, o_fox], axis=1)
    return _matmul(o, w_out.astype(BF16), seq=seq, out_dtype=F32, epilogue="residual",
                   mod=mod, x_res=x, gate_row=2, name="l0_out_proj")


def _mla_down_kernel(x_ref, gain_ref, mod_ref, wd_ref, qn_ref, kvn_ref, cos_ref, up_ref, dn_ref,
                     cq_ref, kin_ref, hb_ref):
    def store(r0, rows, h):
        hb_ref[pl.ds(r0, rows), :] = h.astype(BF16)
    _norm_mod_rows(x_ref, gain_ref, mod_ref, 0, 1, store)
    a = jnp.dot(hb_ref[...], wd_ref[...], preferred_element_type=F32)

    def rms(v, g):
        ms = jnp.mean(v * v, axis=-1, keepdims=True)
        return (v * lax.rsqrt(ms + NORM_EPS)) * g

    q_end = MLA_Q_LORA
    kv_end = MLA_Q_LORA + MLA_KV_LORA
    cq_ref[...] = rms(a[:, :q_end], qn_ref[...]).astype(BF16)
    kin_ref[:, :MLA_KV_LORA] = rms(a[:, q_end:kv_end], kvn_ref[...]).astype(BF16)
    kpe = _rope_group(a[:, kv_end:kv_end + LANES], cos_ref[...], up_ref[...], dn_ref[...],
                      MLA_ROPE_DIM // 2)
    kin_ref[:, MLA_KV_LORA:] = kpe.astype(BF16)


def _mla_down(x, gain, mod, w_down, q_norm, kv_norm, tables, seq):
    n, d = x.shape
    tm = _pick(seq, 512)
    tiles_per_seq = seq // tm
    wcols = w_down.shape[1]
    const = lambda i: (0, 0)
    return pl.pallas_call(
        _mla_down_kernel,
        out_shape=[jax.ShapeDtypeStruct((n, MLA_Q_LORA), BF16),
                   jax.ShapeDtypeStruct((n, MLA_KV_LORA + LANES), BF16)],
        grid=(n // tm,),
        in_specs=[pl.BlockSpec((tm, d), lambda i: (i, 0)),
                  pl.BlockSpec((1, d), const),
                  pl.BlockSpec((1, 6, d), lambda i: (i // tiles_per_seq, 0, 0)),
                  pl.BlockSpec((d, wcols), const),
                  pl.BlockSpec((1, MLA_Q_LORA), const),
                  pl.BlockSpec((1, MLA_KV_LORA), const)]
                 + [pl.BlockSpec((tm, LANES), lambda i: (i % tiles_per_seq, 0))] * 3,
        out_specs=[pl.BlockSpec((tm, MLA_Q_LORA), lambda i: (i, 0)),
                   pl.BlockSpec((tm, MLA_KV_LORA + LANES), lambda i: (i, 0))],
        scratch_shapes=[pltpu.VMEM((tm, d), BF16)],
        compiler_params=_cparams(("parallel",), 40),
        name="mla_down",
    )(x, gain.reshape(1, d), mod, w_down, q_norm.reshape(1, -1), kv_norm.reshape(1, -1), *tables)


def _mla_mixer(x, gain, mod, w_dq, q_norm, w_uq, w_dkv, kv_norm, w_ukv, w_out, batch, seq):
    n, d = x.shape
    dk = 2 * LANES
    pad_rope = LANES - MLA_ROPE_DIM
    qk_scale = (MLA_NOPE_DIM + MLA_ROPE_DIM) ** -0.5 * LOG2E
    tables = _rope_tables(seq, MLA_ROPE_DIM)
    w_down = jnp.concatenate(
        [w_dq, w_dkv, jnp.zeros((d, pad_rope), F32)], axis=1).astype(BF16)
    cq, kin = _mla_down(x, gain, mod, w_down, q_norm, kv_norm, tables, seq)

    w_q = (w_uq * qk_scale).reshape(MLA_Q_LORA, MLA_HEADS, MLA_NOPE_DIM + MLA_ROPE_DIM)
    w_q = jnp.pad(w_q, ((0, 0), (0, 0), (0, pad_rope))).reshape(MLA_Q_LORA, MLA_HEADS * dk)
    tn = 512
    q = _matmul(cq, w_q.astype(BF16), seq=seq, epilogue="rot", rot_tables=tables,
                rot_groups=range(1, tn // LANES, 2), rot_half=MLA_ROPE_DIM // 2, tn=tn,
                name="mla_q_proj")

    w_kv = w_ukv.reshape(MLA_KV_LORA, MLA_HEADS, MLA_NOPE_DIM + MLA_V_DIM)
    w_knope = jnp.pad(w_kv[:, :, :MLA_NOPE_DIM], ((0, LANES), (0, 0), (0, LANES)))
    rope_place = jnp.eye(LANES, dk, k=LANES, dtype=F32) * (
        jnp.arange(LANES) < MLA_ROPE_DIM).astype(F32)[:, None]
    rope_rows = jnp.concatenate([jnp.zeros((MLA_KV_LORA, dk), F32), rope_place], axis=0)
    w_k = w_knope + rope_rows[:, None, :]
    w_v = jnp.pad(w_kv[:, :, MLA_NOPE_DIM:], ((0, LANES), (0, 0), (0, 0)))
    w_kv_all = jnp.concatenate([w_k.reshape(MLA_KV_LORA + LANES, MLA_HEADS * dk),
                                w_v.reshape(MLA_KV_LORA + LANES, MLA_HEADS * MLA_V_DIM)], axis=1)
    kv = _matmul(kin, w_kv_all.astype(BF16), seq=seq, tn=tn, name="mla_kv_proj")

    o = _attention(q, kv, kv, batch=batch, seq=seq, heads=MLA_HEADS, dk=dk, dv=MLA_V_DIM,
                   q_col=0, k_col=0, v_col=MLA_HEADS * dk // MLA_V_DIM, mode="plain",
                   name="mla_attn")
    return _matmul(o, w_out.astype(BF16), seq=seq, out_dtype=F32, epilogue="residual",
                   mod=mod, x_res=x, gate_row=2, name="l1_out_proj")


def kernel(x, c, ada_w, ada_b, norm_mix, norm_ffn, ab_w_in, ab_forget_bias, ab_w_out, mla_w_dq,
           mla_q_norm, mla_w_uq, mla_w_dkv, mla_kv_norm, mla_w_ukv, mla_w_out, router_w,
           router_bias, exp_w_gate, exp_w_up, exp_w_down, final_norm):
    batch, seq, d = x.shape
    depth = ada_w.shape[0]
    mod = _adaln(c, ada_w, ada_b)
    xs = x.reshape(batch * seq, d)
    for layer in range(depth):
        if layer % 2 == 0:
            xs = _moba_fox_mixer(xs, norm_mix[layer], mod[layer], ab_w_in, ab_forget_bias,
                                 ab_w_out, batch, seq)
        else:
            xs = _mla_mixer(xs, norm_mix[layer], mod[layer], mla_w_dq, mla_q_norm, mla_w_uq,
                            mla_w_dkv, mla_kv_norm, mla_w_ukv, mla_w_out, batch, seq)
        final_gain = final_norm if layer == depth - 1 else None
        xs = _moe_layer(xs, norm_ffn[layer], mod[layer], router_w, router_bias,
                        exp_w_gate, exp_w_up, exp_w_down, layer, seq, final_gain)
    return xs.reshape(batch, seq, d)
```

```python
import functools
import math

import jax
import jax.numpy as jnp
from jax import lax
from jax.experimental import pallas as pl
from jax.experimental.pallas import tpu as pltpu

F32 = jnp.float32
BF16 = jnp.bfloat16

HEAD_DIM = 128
MOBA_HEADS = 8
FOX_HEADS = 8
ROT_DIM = HEAD_DIM // 4
ROPE_THETA = 500000.0
MOBA_BLOCK = 256
MOBA_TOPK = 3
MLA_HEADS = 16
MLA_Q_LORA = 512
MLA_KV_LORA = 512
MLA_NOPE_DIM = 128
MLA_ROPE_DIM = 64
MLA_V_DIM = 128
N_EXPERTS = 16
N_GROUPS = 4
EXPERTS_PER_GROUP = N_EXPERTS // N_GROUPS
TOP_K = 2
NORM_EPS = 1e-6

LANES = 128
V7X_VMEM_BYTES = 64 * 1024 * 1024
MIB = 1024 * 1024
LOG2E = math.log2(math.e)

NT_DIMS = (((1,), (1,)), ((), ()))


def _cparams(semantics, vmem_mib):
    assert vmem_mib * MIB < V7X_VMEM_BYTES
    return pltpu.CompilerParams(dimension_semantics=semantics,
                                vmem_limit_bytes=vmem_mib * MIB)


def _pick(n, pref):
    t = min(pref, n)
    while n % t:
        t //= 2
    return t


def _adaln_kernel(c_ref, w_ref, b_ref, o_ref):
    c = c_ref[...]
    ca = (c * jax.nn.sigmoid(c)).astype(BF16)
    acc = jnp.dot(ca, w_ref[0].astype(BF16), preferred_element_type=F32)
    o_ref[0] = acc + b_ref[0]


def _adaln(c, ada_w, ada_b):
    depth, d, n6 = ada_w.shape
    b = c.shape[0]
    rows = 8
    c_pad = jnp.pad(c, ((0, rows - b), (0, 0)))
    tn = _pick(n6, 1024)
    out = pl.pallas_call(
        _adaln_kernel,
        out_shape=jax.ShapeDtypeStruct((depth, rows, n6), F32),
        grid=(depth, n6 // tn),
        in_specs=[
            pl.BlockSpec((rows, d), lambda l, j: (0, 0)),
            pl.BlockSpec((1, d, tn), lambda l, j: (l, 0, j)),
            pl.BlockSpec((1, 1, tn), lambda l, j: (l, 0, j)),
        ],
        out_specs=pl.BlockSpec((1, rows, tn), lambda l, j: (l, 0, j)),
        compiler_params=_cparams(("parallel", "parallel"), 40),
        name="adaln_mod",
    )(c_pad, ada_w, ada_b.reshape(depth, 1, n6))
    return out[:, :b, :].reshape(depth, b, 6, d)


NORM_ROWS = 256


def _norm_mod_rows(x_ref, gain_ref, mod_ref, shift_row, scale_row, store):
    tm = x_ref.shape[0]
    rows = min(NORM_ROWS, tm)
    gain = gain_ref[...]
    scale1p = 1.0 + mod_ref[0, scale_row:scale_row + 1, :]
    shift = mod_ref[0, shift_row:shift_row + 1, :]

    def body(c, carry):
        r0 = pl.multiple_of(c * rows, rows)
        x = x_ref[pl.ds(r0, rows), :]
        ms = jnp.mean(x * x, axis=-1, keepdims=True)
        y = x * lax.rsqrt(ms + NORM_EPS)
        h = (y * gain) * scale1p + shift
        store(r0, rows, h)
        return carry

    lax.fori_loop(0, tm // rows, body, 0)


def _rope_group(a, cos_t, sin_up, sin_dn, half):
    up = pltpu.roll(a, LANES - half, 1)
    dn = pltpu.roll(a, half, 1)
    return a * cos_t + up * sin_up + dn * sin_dn


def _rope_tables(seq, rot_dim):
    half = rot_dim // 2
    inv_freq = ROPE_THETA ** (-jnp.arange(half, dtype=F32) / half)
    ang = jnp.arange(seq, dtype=F32)[:, None] * inv_freq[None, :]
    cos, sin = jnp.cos(ang), jnp.sin(ang)
    zeros = jnp.zeros((seq, LANES - rot_dim), F32)
    zh = jnp.zeros((seq, half), F32)
    cos_t = jnp.concatenate([cos, cos, jnp.ones((seq, LANES - rot_dim), F32)], axis=1)
    sin_up = jnp.concatenate([-sin, zh, zeros], axis=1)
    sin_dn = jnp.concatenate([zh, sin, zeros], axis=1)
    return cos_t, sin_up, sin_dn


def _mm_kernel(*refs, prologue, epilogue, rot_groups, rot_tile_limit, rot_half,
               shift_row, scale_row, gate_row, has_side):
    it = iter(refs)
    a_ref = next(it)
    if prologue == "norm":
        gain_ref = next(it)
    if prologue == "norm" or epilogue == "residual":
        mod_ref = next(it)
    w_ref = next(it)
    if has_side:
        wside_ref = next(it)
    if epilogue == "rot":
        cos_ref, up_ref, dn_ref = next(it), next(it), next(it)
    if epilogue == "residual":
        x_ref = next(it)
    o_ref = next(it)
    if has_side:
        side_ref = next(it)
    if prologue == "norm":
        hb_ref = next(it)

    j = pl.program_id(1)

    if prologue == "norm":
        @pl.when(j == 0)
        def _():
            def store(r0, rows, h):
                hb_ref[pl.ds(r0, rows), :] = h.astype(BF16)
            _norm_mod_rows(a_ref, gain_ref, mod_ref, shift_row, scale_row, store)
            if has_side:
                side_ref[...] = jnp.dot(hb_ref[...], wside_ref[...],
                                        preferred_element_type=F32)
        lhs_ref = hb_ref
    else:
        lhs_ref = a_ref

    def product():
        return jnp.dot(lhs_ref[...], w_ref[...], preferred_element_type=F32)

    if epilogue == "rot":
        def rotated():
            acc = product()
            groups = []
            for g in range(acc.shape[1] // LANES):
                blk = acc[:, g * LANES:(g + 1) * LANES]
                if g in rot_groups:
                    blk = _rope_group(blk, cos_ref[...], up_ref[...], dn_ref[...], rot_half)
                groups.append(blk)
            return jnp.concatenate(groups, axis=1)

        if rot_tile_limit is None:
            o_ref[...] = rotated().astype(o_ref.dtype)
        else:
            @pl.when(j < rot_tile_limit)
            def _():
                o_ref[...] = rotated().astype(o_ref.dtype)

            @pl.when(j >= rot_tile_limit)
            def _():
                o_ref[...] = product().astype(o_ref.dtype)
    elif epilogue == "residual":
        gate = mod_ref[0, gate_row:gate_row + 1, :]
        o_ref[...] = x_ref[...] + gate * product()
    else:
        o_ref[...] = product().astype(o_ref.dtype)


def _matmul(a, w, *, seq, out_dtype=BF16, tm=1024, tn=512, prologue="none", gain=None,
            mod=None, shift_row=0, scale_row=1, epilogue="none", rot_tables=None,
            rot_groups=(), rot_tile_limit=None, rot_half=0, x_res=None, gate_row=2,
            w_side=None, name="matmul"):
    m, k = a.shape
    n = w.shape[1]
    tm = _pick(seq, tm)
    tn = _pick(n, tn)
    tiles_per_seq = seq // tm
    has_side = w_side is not None
    in_specs = [pl.BlockSpec((tm, k), lambda i, j: (i, 0))]
    args = [a]
    if prologue == "norm":
        in_specs.append(pl.BlockSpec((1, k), lambda i, j: (0, 0)))
        args.append(gain.reshape(1, k))
    if prologue == "norm":
        in_specs.append(pl.BlockSpec((1, 6, k), lambda i, j: (i // tiles_per_seq, 0, 0)))
        args.append(mod)
    elif epilogue == "residual":
        in_specs.append(pl.BlockSpec((1, 6, tn), lambda i, j: (i // tiles_per_seq, 0, j)))
        args.append(mod)
    in_specs.append(pl.BlockSpec((k, tn), lambda i, j: (0, j)))
    args.append(w)
    if has_side:
        in_specs.append(pl.BlockSpec((k, LANES), lambda i, j: (0, 0)))
        args.append(w_side)
    if epilogue == "rot":
        for t in rot_tables:
            in_specs.append(pl.BlockSpec((tm, LANES), lambda i, j: (i % tiles_per_seq, 0)))
            args.append(t)
    if epilogue == "residual":
        in_specs.append(pl.BlockSpec((tm, tn), lambda i, j: (i, j)))
        args.append(x_res)
    out_shape = [jax.ShapeDtypeStruct((m, n), out_dtype)]
    out_specs = [pl.BlockSpec((tm, tn), lambda i, j: (i, j))]
    if has_side:
        out_shape.append(jax.ShapeDtypeStruct((m, LANES), F32))
        out_specs.append(pl.BlockSpec((tm, LANES), lambda i, j: (i, 0)))
    scratch = [pltpu.VMEM((tm, k), BF16)] if prologue == "norm" else []
    kernel = functools.partial(
        _mm_kernel, prologue=prologue, epilogue=epilogue, rot_groups=tuple(rot_groups),
        rot_tile_limit=rot_tile_limit, rot_half=rot_half, shift_row=shift_row,
        scale_row=scale_row, gate_row=gate_row, has_side=has_side)
    outs = pl.pallas_call(
        kernel,
        out_shape=out_shape,
        grid=(m // tm, n // tn),
        in_specs=in_specs,
        out_specs=out_specs,
        scratch_shapes=scratch,
        compiler_params=_cparams(("parallel", "arbitrary"), 48),
        name=name,
    )(*args)
    return outs if has_side else outs[0]


def _fox_prep_kernel(flog_ref, bias_ref, qx_ref, kx_ref, carry_ref):
    tc = flog_ref.shape[0]

    @pl.when(pl.program_id(1) == 0)
    def _():
        carry_ref[...] = jnp.zeros_like(carry_ref)

    z = flog_ref[...] + bias_ref[...]
    logf = jnp.minimum(z, 0.0) - jnp.log1p(jnp.exp(-jnp.abs(z)))
    row = lax.broadcasted_iota(jnp.int32, (tc, tc), 0)
    col = lax.broadcasted_iota(jnp.int32, (tc, tc), 1)
    tri = jnp.where(col <= row, 1.0, 0.0).astype(F32)
    csum = jnp.dot(tri, logf, preferred_element_type=F32,
                   precision=lax.Precision.HIGHEST) + carry_ref[...]
    carry_ref[...] = csum[tc - 1:tc, :]
    csum = csum * LOG2E
    lane = lax.broadcasted_iota(jnp.int32, (tc, LANES), 1)
    one = jnp.ones((tc, LANES), F32)
    zero = jnp.zeros((tc, LANES), F32)
    for h in range(FOX_HEADS):
        colv = jnp.broadcast_to(csum[:, h:h + 1], (tc, LANES))
        hi = colv.astype(BF16).astype(F32)
        r1 = colv - hi
        mid = r1.astype(BF16).astype(F32)
        lo = r1 - mid
        qv = jnp.where(lane == 0, hi, jnp.where(lane == 1, mid, jnp.where(
            lane == 2, lo, jnp.where(lane < 6, one, zero))))
        kv = jnp.where(lane < 3, one, jnp.where(lane == 3, -hi, jnp.where(
            lane == 4, -mid, jnp.where(lane == 5, -lo, zero))))
        qx_ref[:, h * LANES:(h + 1) * LANES] = qv.astype(BF16)
        kx_ref[:, h * LANES:(h + 1) * LANES] = kv.astype(BF16)


def _fox_prep(flog, forget_bias, batch, seq):
    n = flog.shape[0]
    tc = _pick(seq, 256)
    bias = jnp.pad(forget_bias, (0, LANES - FOX_HEADS)).reshape(1, LANES)
    width = FOX_HEADS * LANES
    steps = seq // tc
    return pl.pallas_call(
        _fox_prep_kernel,
        out_shape=[jax.ShapeDtypeStruct((n, width), BF16)] * 2,
        grid=(batch, steps),
        in_specs=[pl.BlockSpec((tc, LANES), lambda b, s: (b * steps + s, 0)),
                  pl.BlockSpec((1, LANES), lambda b, s: (0, 0))],
        out_specs=[pl.BlockSpec((tc, width), lambda b, s: (b * steps + s, 0))] * 2,
        scratch_shapes=[pltpu.VMEM((1, LANES), F32)],
        compiler_params=_cparams(("parallel", "arbitrary"), 32),
        name="fox_prep",
    )(flog, bias)


ATTN_TILE = 512
ATTN_HEADS_PER_STEP = 8


M_INIT = -1e30


def _attn_kernel(qt_ref, kt_ref, *refs, mode, tq, dk, dv):
    it = iter(refs)
    q_ref, k_ref, v_ref = next(it), next(it), next(it)
    if mode == "fox":
        qx_ref, kx_ref = next(it), next(it)
    if mode == "moba":
        kmean_ref = next(it)
    o_ref = next(it)
    m_ref, l_ref, acc_ref = next(it), next(it), next(it)
    if mode == "moba":
        sel_ref = next(it)

    p_id = pl.program_id(2)
    i, j = qt_ref[p_id], kt_ref[p_id]
    neg_inf = jnp.float32(-jnp.inf)
    heads = q_ref.shape[1] // dk
    blocks_per_tile = tq // MOBA_BLOCK
    block_shift = MOBA_BLOCK.bit_length() - 1

    def q_of(h):
        q = q_ref[:, h * dk:(h + 1) * dk]
        if mode == "fox":
            q = jnp.concatenate([q, qx_ref[:, h * LANES:(h + 1) * LANES]], axis=1)
        return q

    def scores(h):
        k = k_ref[:, h * dk:(h + 1) * dk]
        if mode == "fox":
            k = jnp.concatenate([k, kx_ref[:, h * LANES:(h + 1) * LANES]], axis=1)
        return lax.dot_general(q_of(h), k, NT_DIMS, preferred_element_type=F32)

    def block_hits(h):
        lane = lax.broadcasted_iota(jnp.int32, (tq, LANES), 1)
        sel = sel_ref[h]
        return [jnp.broadcast_to(
            jnp.max(jnp.where(lane == j * blocks_per_tile + c, sel, 0.0), axis=1, keepdims=True),
            (tq, LANES)) > 0.0 for c in range(blocks_per_tile)]

    def masked(s, allowed_of_group):
        groups = [jnp.where(allowed_of_group(g), s[:, g * LANES:(g + 1) * LANES], neg_inf)
                  for g in range(tq // LANES)]
        return jnp.concatenate(groups, axis=1)

    def update(h, s):
        m_old = m_ref[h]
        row_max = jnp.broadcast_to(jnp.max(s, axis=1, keepdims=True), m_old.shape)
        m_new = jnp.maximum(m_old, row_max)
        alpha = jnp.exp2(m_old - m_new)
        p = jnp.exp2(s - jnp.concatenate([m_new] * (s.shape[1] // LANES), axis=1))
        row_sum = jnp.broadcast_to(jnp.sum(p, axis=1, keepdims=True), m_old.shape)
        l_ref[h] = alpha * l_ref[h] + row_sum
        acc_ref[h] = alpha * acc_ref[h] + jnp.dot(
            p.astype(BF16), v_ref[:, h * dv:(h + 1) * dv], preferred_element_type=F32)
        m_ref[h] = m_new

    @pl.when(j == 0)
    def _():
        m_ref[...] = jnp.full_like(m_ref, M_INIT)
        l_ref[...] = jnp.zeros_like(l_ref)
        acc_ref[...] = jnp.zeros_like(acc_ref)
        if mode == "moba":
            nb = kmean_ref.shape[0]
            lane = lax.broadcasted_iota(jnp.int32, (tq, LANES), 1)
            lane_f = lane.astype(F32)
            row_block = jnp.right_shift(lax.broadcasted_iota(jnp.int32, (tq, LANES), 0),
                                        block_shift)
            past = lane < i * blocks_per_tile + row_block
            for h in range(heads):
                kmean = jnp.concatenate(
                    [kmean_ref[:, h * dk:(h + 1) * dk], jnp.zeros((LANES - nb, dk), F32)], axis=0)
                gate = lax.dot_general(q_of(h).astype(F32), kmean, NT_DIMS,
                                       preferred_element_type=F32,
                                       precision=lax.Precision.HIGHEST)
                g = jnp.where(past, gate, neg_inf)
                chosen = jnp.zeros(gate.shape, F32)
                for _ in range(min(MOBA_TOPK, nb - 1)):
                    mx = jnp.broadcast_to(jnp.max(g, axis=1, keepdims=True), g.shape)
                    first = jnp.broadcast_to(
                        jnp.min(jnp.where(g == mx, lane_f, float(LANES)), axis=1, keepdims=True),
                        g.shape)
                    pick = lane_f == first
                    chosen = jnp.where(jnp.logical_and(pick, past), 1.0, chosen)
                    g = jnp.where(pick, neg_inf, g)
                sel_ref[h] = chosen

    groups_per_block = MOBA_BLOCK // LANES

    @pl.when(j < i)
    def _():
        for h in range(heads):
            s = scores(h)
            if mode == "moba":
                hits = block_hits(h)
                s = masked(s, lambda g: hits[g // groups_per_block])
            update(h, s)

    @pl.when(j == i)
    def _():
        row = lax.broadcasted_iota(jnp.int32, (tq, LANES), 0)
        lane = lax.broadcasted_iota(jnp.int32, (tq, LANES), 1)
        row_blk = jnp.right_shift(row, block_shift)
        for h in range(heads):
            s = scores(h)
            if mode == "moba":
                hits = block_hits(h)

                def allowed(g):
                    col_blk = g // groups_per_block
                    own = jnp.logical_and(row_blk == col_blk, lane + g * LANES <= row)
                    return jnp.logical_or(own, jnp.logical_and(row_blk > col_blk, hits[col_blk]))
            else:
                def allowed(g):
                    return lane + g * LANES <= row
            update(h, masked(s, allowed))
            o_ref[:, h * dv:(h + 1) * dv] = (acc_ref[h] / l_ref[h]).astype(o_ref.dtype)


def _attention(q_arr, k_arr, v_arr, *, batch, seq, heads, dk, dv, q_col, k_col, v_col, mode,
               qx=None, kx=None, kmean=None, name="attn"):
    n = batch * seq
    tq = _pick(seq, ATTN_TILE)
    g = ATTN_HEADS_PER_STEP
    assert tq % MOBA_BLOCK == 0 and heads % g == 0
    assert q_col % g == 0 and k_col % g == 0 and v_col % g == 0
    nq = seq // tq
    pairs = [(i, j) for i in range(nq) for j in range(i + 1)]
    q_tile = jnp.array([p[0] for p in pairs], jnp.int32)
    k_tile = jnp.array([p[1] for p in pairs], jnp.int32)

    def q_rows(b, h, p, qt, kt):
        return b * nq + qt[p]

    def k_rows(b, h, p, qt, kt):
        return b * nq + kt[p]

    in_specs = [
        pl.BlockSpec((tq, g * dk), lambda b, h, p, qt, kt: (q_rows(b, h, p, qt, kt), q_col // g + h)),
        pl.BlockSpec((tq, g * dk), lambda b, h, p, qt, kt: (k_rows(b, h, p, qt, kt), k_col // g + h)),
        pl.BlockSpec((tq, g * dv), lambda b, h, p, qt, kt: (k_rows(b, h, p, qt, kt), v_col // g + h)),
    ]
    args = [q_arr, k_arr, v_arr]
    if mode == "fox":
        in_specs += [
            pl.BlockSpec((tq, g * LANES), lambda b, h, p, qt, kt: (q_rows(b, h, p, qt, kt), h)),
            pl.BlockSpec((tq, g * LANES), lambda b, h, p, qt, kt: (k_rows(b, h, p, qt, kt), h))]
        args += [qx, kx]
    if mode == "moba":
        nb = seq // MOBA_BLOCK
        in_specs.append(pl.BlockSpec((nb, g * dk), lambda b, h, p, qt, kt: (b, h)))
        args.append(kmean)
    assert dv == LANES
    scratch = [pltpu.VMEM((g, tq, LANES), F32), pltpu.VMEM((g, tq, LANES), F32),
               pltpu.VMEM((g, tq, dv), F32)]
    if mode == "moba":
        scratch.append(pltpu.VMEM((g, tq, LANES), F32))
    return pl.pallas_call(
        functools.partial(_attn_kernel, mode=mode, tq=tq, dk=dk, dv=dv),
        out_shape=jax.ShapeDtypeStruct((n, heads * dv), BF16),
        grid_spec=pltpu.PrefetchScalarGridSpec(
            num_scalar_prefetch=2,
            grid=(batch, heads // g, len(pairs)),
            in_specs=in_specs,
            out_specs=pl.BlockSpec((tq, g * dv),
                                   lambda b, h, p, qt, kt: (q_rows(b, h, p, qt, kt), h)),
            scratch_shapes=scratch,
        ),
        compiler_params=_cparams(("parallel", "parallel", "arbitrary"), 40),
        name=name,
    )(q_tile, k_tile, *args)


def _block_means_kernel(k_ref, o_ref):
    for b in range(o_ref.shape[0]):
        kb = k_ref[b * MOBA_BLOCK:(b + 1) * MOBA_BLOCK, :].astype(F32)
        o_ref[b:b + 1, :] = jnp.mean(kb, axis=0, keepdims=True)


def _block_means(arr, col0, cols):
    n = arr.shape[0]
    blocks = 8
    rows = blocks * MOBA_BLOCK
    assert n % rows == 0 and col0 % cols == 0
    return pl.pallas_call(
        _block_means_kernel,
        out_shape=jax.ShapeDtypeStruct((n // MOBA_BLOCK, cols), F32),
        grid=(n // rows,),
        in_specs=[pl.BlockSpec((rows, cols), lambda i: (i, col0 // cols))],
        out_specs=pl.BlockSpec((blocks, cols), lambda i: (i, 0)),
        compiler_params=_cparams(("parallel",), 32),
        name="moba_block_means",
    )(arr)


def _top2_of4(vals):
    a, b, c, d = vals
    hi1, lo1 = jnp.maximum(a, b), jnp.minimum(a, b)
    hi2, lo2 = jnp.maximum(c, d), jnp.minimum(c, d)
    return jnp.maximum(hi1, hi2) + jnp.maximum(jnp.minimum(hi1, hi2), jnp.maximum(lo1, lo2))


def _route_kernel(x_ref, gain_ref, mod_ref, rwt_ref, rbias_ref, h_ref, idx_ref, wgt_ref):
    def store(r0, rows, h):
        h_ref[pl.ds(r0, rows), :] = h
    _norm_mod_rows(x_ref, gain_ref, mod_ref, 3, 4, store)

    logits = lax.dot_general(rwt_ref[...], h_ref[...], NT_DIMS, preferred_element_type=F32,
                             precision=lax.Precision.HIGHEST)
    score = jax.nn.sigmoid(logits)
    biased = score + rbias_ref[...]
    rows_b = [biased[e:e + 1, :] for e in range(N_EXPERTS)]
    rows_s = [score[e:e + 1, :] for e in range(N_EXPERTS)]
    grp = [_top2_of4(rows_b[g * EXPERTS_PER_GROUP:(g + 1) * EXPERTS_PER_GROUP])
           for g in range(N_GROUPS)]
    best, g_sel = grp[0], jnp.zeros_like(grp[0], dtype=jnp.int32)
    for g in range(1, N_GROUPS):
        better = grp[g] > best
        best = jnp.where(better, grp[g], best)
        g_sel = jnp.where(better, g, g_sel)

    def in_group(rows, r):
        out = rows[r]
        for g in range(1, N_GROUPS):
            out = jnp.where(g_sel == g, rows[g * EXPERTS_PER_GROUP + r], out)
        return out

    cb = [in_group(rows_b, r) for r in range(EXPERTS_PER_GROUP)]
    cs = [in_group(rows_s, r) for r in range(EXPERTS_PER_GROUP)]
    neg_inf = jnp.float32(-jnp.inf)
    picks = []
    for _ in range(TOP_K):
        bv, bi, bs = cb[0], jnp.zeros_like(g_sel), cs[0]
        for r in range(1, EXPERTS_PER_GROUP):
            better = cb[r] > bv
            bv = jnp.where(better, cb[r], bv)
            bi = jnp.where(better, r, bi)
            bs = jnp.where(better, cs[r], bs)
        picks.append((bi, bs))
        cb = [jnp.where(bi == r, neg_inf, cb[r]) for r in range(EXPERTS_PER_GROUP)]
    (i0, s0), (i1, s1) = picks
    total = s0 + s1
    idx_ref[0:1, :] = g_sel * EXPERTS_PER_GROUP + i0
    idx_ref[1:2, :] = g_sel * EXPERTS_PER_GROUP + i1
    wgt_ref[0:1, :] = s0 / total
    wgt_ref[1:2, :] = s1 / total


def _route(x, gain, mod, router_w, router_bias, seq):
    n, d = x.shape
    tm = _pick(seq, 512)
    tiles_per_seq = seq // tm
    return pl.pallas_call(
        _route_kernel,
        out_shape=[jax.ShapeDtypeStruct((n, d), F32),
                   jax.ShapeDtypeStruct((TOP_K, n), jnp.int32),
                   jax.ShapeDtypeStruct((TOP_K, n), F32)],
        grid=(n // tm,),
        in_specs=[pl.BlockSpec((tm, d), lambda i: (i, 0)),
                  pl.BlockSpec((1, d), lambda i: (0, 0)),
                  pl.BlockSpec((1, 6, d), lambda i: (i // tiles_per_seq, 0, 0)),
                  pl.BlockSpec((N_EXPERTS, d), lambda i: (0, 0)),
                  pl.BlockSpec((N_EXPERTS, 1), lambda i: (0, 0))],
        out_specs=[pl.BlockSpec((tm, d), lambda i: (i, 0)),
                   pl.BlockSpec((TOP_K, tm), lambda i: (0, i)),
                   pl.BlockSpec((TOP_K, tm), lambda i: (0, i))],
        compiler_params=_cparams(("parallel",), 32),
        name="moe_route",
    )(x, gain.reshape(1, d), mod, router_w.T, router_bias.reshape(N_EXPERTS, 1))


GMM_ROWS = 512
GMM_UP_COLS = 128
GMM_DOWN_COLS = 512
GMM_ISSUE_STEPS = 8
PACK_ROWS = 256
DMA_UNROLL = 8


def _pack_kernel(wg_ref, wu_ref, wd_ref, wgu_ref, wdb_ref):
    n_up = wgu_ref.shape[1]
    for f in range(n_up):
        cols = slice(f * GMM_UP_COLS, (f + 1) * GMM_UP_COLS)
        wgu_ref[0, f, :, :GMM_UP_COLS] = wg_ref[0, 0, :, cols].astype(BF16)
        wgu_ref[0, f, :, GMM_UP_COLS:] = wu_ref[0, 0, :, cols].astype(BF16)
    wdb_ref[0] = wd_ref[0, 0].astype(BF16)


def _pack_expert_weights(w_gate, w_up, w_down, layer):
    _, n_exp, d, d_exp = w_gate.shape
    n_up = d_exp // GMM_UP_COLS
    steps = d // PACK_ROWS
    down_rows = d_exp // steps
    assert d_exp % GMM_UP_COLS == 0 and d % PACK_ROWS == 0 and down_rows % 16 == 0
    up_spec = pl.BlockSpec((1, 1, PACK_ROWS, d_exp), lambda e, k: (layer, e, k, 0))
    return pl.pallas_call(
        _pack_kernel,
        out_shape=[jax.ShapeDtypeStruct((n_exp, n_up, d, 2 * GMM_UP_COLS), BF16),
                   jax.ShapeDtypeStruct((n_exp, d_exp, d), BF16)],
        grid=(n_exp, steps),
        in_specs=[up_spec, up_spec,
                  pl.BlockSpec((1, 1, down_rows, d), lambda e, k: (layer, e, k, 0))],
        out_specs=[pl.BlockSpec((1, n_up, PACK_ROWS, 2 * GMM_UP_COLS), lambda e, k: (e, 0, k, 0)),
                   pl.BlockSpec((1, down_rows, d), lambda e, k: (e, k, 0))],
        compiler_params=_cparams(("parallel", "parallel"), 32),
        name="moe_pack_weights",
    )(w_gate, w_up, w_down)


def _ffn_kernel(eid_ref, nvalid_ref, dst_ref, h_hbm, wgu_ref, wd_ref, y_hbm,
                xbuf_ref, hb_ref, act_ref, ystage_ref, gsem, ssem):
    t = pl.program_id(0)
    rows = hb_ref.shape[0]
    n_up = act_ref.shape[0]
    n_tokens = h_hbm.shape[0]
    per_chunk = rows // GMM_ISSUE_STEPS
    nv = nvalid_ref[0]
    valid = t < nv

    def dst_of(tile, r):
        return dst_ref[(tile + 1) * rows + r]

    def token_of(dst_row):
        if n_tokens & (n_tokens - 1) == 0:
            return jnp.bitwise_and(dst_row, n_tokens - 1)
        return lax.rem(dst_row, n_tokens)

    def gather_row(r, src_row):
        return pltpu.make_async_copy(h_hbm.at[pl.ds(src_row, 1), :],
                                     xbuf_ref.at[pl.ds(r, 1), :], gsem)

    def scatter_row(r, dst_row):
        return pltpu.make_async_copy(ystage_ref.at[pl.ds(r, 1), :],
                                     y_hbm.at[pl.ds(dst_row, 1), :], ssem)

    def for_rows(fn):
        def body(r, carry):
            fn(r)
            return carry
        lax.fori_loop(0, rows, body, 0, unroll=DMA_UNROLL)

    @pl.when(t == 0)
    def _():
        ystage_ref[...] = jnp.zeros_like(ystage_ref)
        for_rows(lambda r: gather_row(r, token_of(dst_of(0, r))).start())

    @pl.when(t <= nv)
    def _():
        for_rows(lambda r: gather_row(r, 0).wait())

    @pl.when(t == nv)
    def _():
        for_rows(lambda r: scatter_row(r, dst_of(t - 1, r)).start())
        for_rows(lambda r: scatter_row(r, 0).wait())

    @pl.when(valid)
    def _():
        hb_ref[...] = xbuf_ref[...].astype(BF16)
        for f in range(n_up):
            if f < GMM_ISSUE_STEPS:
                for r in range(f * per_chunk, (f + 1) * per_chunk):
                    gather_row(r, token_of(dst_of(t + 1, r))).start()
                    scatter_row(r, dst_of(t - 1, r)).start()
            gu = jnp.dot(hb_ref[...], wgu_ref[0, f], preferred_element_type=F32)
            g, u = gu[:, :GMM_UP_COLS], gu[:, GMM_UP_COLS:]
            act_ref[f] = ((g * jax.nn.sigmoid(g)) * u).astype(BF16)
        for_rows(lambda r: scatter_row(r, 0).wait())
        act = jnp.concatenate([act_ref[k] for k in range(n_up)], axis=1)
        ystage_ref[...] = jnp.dot(act, wd_ref[0], preferred_element_type=F32)


def _expert_ffn(eid, nvalid, dst, h, wgu, wdb, n_slots):
    n_tiles = dst.shape[0] // GMM_ROWS - 1
    d = h.shape[1]
    n_exp, n_up, _, up_cols = wgu.shape
    d_exp = wdb.shape[1]
    assert GMM_ROWS % GMM_ISSUE_STEPS == 0 and n_up >= GMM_ISSUE_STEPS
    return pl.pallas_call(
        _ffn_kernel,
        out_shape=jax.ShapeDtypeStruct((n_slots + GMM_ROWS, d), F32),
        grid_spec=pltpu.PrefetchScalarGridSpec(
            num_scalar_prefetch=3,
            grid=(n_tiles,),
            in_specs=[
                pl.BlockSpec(memory_space=pl.ANY),
                pl.BlockSpec((1, n_up, d, up_cols), lambda t, e, nv, ds: (e[t], 0, 0, 0)),
                pl.BlockSpec((1, d_exp, d), lambda t, e, nv, ds: (e[t], 0, 0)),
            ],
            out_specs=pl.BlockSpec(memory_space=pl.ANY),
            scratch_shapes=[pltpu.VMEM((GMM_ROWS, d), F32),
                            pltpu.VMEM((GMM_ROWS, d), BF16),
                            pltpu.VMEM((n_up, GMM_ROWS, GMM_UP_COLS), BF16),
                            pltpu.VMEM((GMM_ROWS, d), F32),
                            pltpu.SemaphoreType.DMA,
                            pltpu.SemaphoreType.DMA],
        ),
        compiler_params=_cparams(("arbitrary",), 56),
        name="moe_ffn",
    )(eid, nvalid, dst, h, wgu, wdb)


def _combine_kernel(x_ref, y0_ref, y1_ref, w_ref, mod_ref, *rest, final):
    if final:
        gain_ref, o_ref = rest
    else:
        (o_ref,) = rest
    rows = min(NORM_ROWS, x_ref.shape[0])
    gate = mod_ref[0, 5:6, :]

    def body(c, carry):
        r0 = pl.multiple_of(c * rows, rows)
        w = w_ref[pl.ds(r0, rows), :]
        y = (w[:, 0:1] * y0_ref[pl.ds(r0, rows), :]
             + w[:, 1:2] * y1_ref[pl.ds(r0, rows), :])
        out = x_ref[pl.ds(r0, rows), :] + gate * y
        if final:
            ms = jnp.mean(out * out, axis=-1, keepdims=True)
            out = (out * lax.rsqrt(ms + NORM_EPS)) * gain_ref[...]
        o_ref[pl.ds(r0, rows), :] = out
        return carry

    lax.fori_loop(0, x_ref.shape[0] // rows, body, 0)


def _combine(x, y_slots, wgt, mod, seq, final_gain=None):
    n, d = x.shape
    tm = _pick(seq, 512)
    tiles_per_seq = seq // tm
    tiles = n // tm
    final = final_gain is not None
    in_specs = [pl.BlockSpec((tm, d), lambda i: (i, 0)),
                pl.BlockSpec((tm, d), lambda i: (i, 0)),
                pl.BlockSpec((tm, d), lambda i: (tiles + i, 0)),
                pl.BlockSpec((tm, TOP_K), lambda i: (i, 0)),
                pl.BlockSpec((1, 6, d), lambda i: (i // tiles_per_seq, 0, 0))]
    args = [x, y_slots, y_slots, wgt, mod]
    if final:
        in_specs.append(pl.BlockSpec((1, d), lambda i: (0, 0)))
        args.append(final_gain.reshape(1, d))
    return pl.pallas_call(
        functools.partial(_combine_kernel, final=final),
        out_shape=jax.ShapeDtypeStruct((n, d), F32),
        grid=(tiles,),
        in_specs=in_specs,
        out_specs=pl.BlockSpec((tm, d), lambda i: (i, 0)),
        compiler_params=_cparams(("parallel",), 48),
        name="moe_combine",
    )(*args)


def _dispatch_plan(idx):
    n = idx.shape[1]
    slots = TOP_K * n
    cap = slots + N_EXPERTS * GMM_ROWS
    expert = idx.reshape(slots)
    onehot = (expert[:, None] == jnp.arange(N_EXPERTS)[None, :]).astype(jnp.int32)
    csum = jnp.cumsum(onehot, axis=0)
    rank = jnp.sum(csum * onehot, axis=1) - 1
    counts = csum[-1]
    padded = -(-counts // GMM_ROWS) * GMM_ROWS
    ends = jnp.cumsum(padded)
    pos = (ends - padded)[expert] + rank
    slot_ids = jnp.arange(slots, dtype=jnp.int32)
    dummy = slots + jnp.arange(cap + GMM_ROWS, dtype=jnp.int32) % GMM_ROWS
    dst = dummy.at[pos + GMM_ROWS].set(slot_ids)
    nvalid = (ends[-1] // GMM_ROWS).astype(jnp.int32)
    tile_start = jnp.arange(cap // GMM_ROWS, dtype=jnp.int32) * GMM_ROWS
    tile_start = jnp.minimum(tile_start, ends[-1] - GMM_ROWS)
    eid = jnp.sum(tile_start[:, None] >= ends[None, :], axis=1).astype(jnp.int32)
    return dst, eid, nvalid.reshape(1)


def _moe_layer(x, gain, mod, router_w, router_bias, w_gate, w_up, w_down, layer, seq,
               final_gain):
    wgu, wdb = _pack_expert_weights(w_gate, w_up, w_down, layer)
    h, idx, wgt = _route(x, gain, mod, router_w, router_bias, seq)
    dst, eid, nvalid = _dispatch_plan(idx)
    y_slots = _expert_ffn(eid, nvalid, dst, h, wgu, wdb, TOP_K * x.shape[0])
    return _combine(x, y_slots, wgt.T, mod, seq, final_gain)


def _moba_fox_mixer(x, gain, mod, w_in, forget_bias, w_out, batch, seq):
    n, d = x.shape
    mix = (MOBA_HEADS + FOX_HEADS) * HEAD_DIM
    qk_scale = HEAD_DIM ** -0.5 * LOG2E
    s1 = MOBA_HEADS * HEAD_DIM
    s2 = FOX_HEADS * HEAD_DIM
    col_scale = jnp.ones((3 * mix,), F32)
    col_scale = col_scale.at[:s1].set(qk_scale).at[3 * s1:3 * s1 + s2].set(qk_scale)
    w_main = (w_in[:, :3 * mix] * col_scale[None, :]).astype(BF16)
    w_forget = jnp.pad(w_in[:, 3 * mix:], ((0, 0), (0, LANES - FOX_HEADS))).astype(BF16)
    tn = 512
    proj, flog = _matmul(
        x, w_main, seq=seq, prologue="norm", gain=gain, mod=mod, shift_row=0, scale_row=1,
        epilogue="rot", rot_tables=_rope_tables(seq, ROT_DIM), rot_groups=range(tn // LANES),
        rot_tile_limit=2 * s1 // tn, rot_half=ROT_DIM // 2, w_side=w_forget, tn=tn,
        name="l0_in_proj")
    qx, kx = _fox_prep(flog, forget_bias, batch, seq)
    o_moba = _attention(proj, proj, proj, batch=batch, seq=seq, heads=MOBA_HEADS,
                        dk=HEAD_DIM, dv=HEAD_DIM, q_col=0, k_col=MOBA_HEADS,
                        v_col=2 * MOBA_HEADS, mode="moba", kmean=_block_means(proj, s1, s1),
                        name="moba_attn")
    base = 3 * MOBA_HEADS
    o_fox = _attention(proj, proj, proj, batch=batch, seq=seq, heads=FOX_HEADS,
                       dk=HEAD_DIM, dv=HEAD_DIM, q_col=base, k_col=base + FOX_HEADS,
                       v_col=base + 2 * FOX_HEADS, mode="fox", qx=qx, kx=kx,
                       name="fox_attn")
    o = jnp.concatenate([o_moba, o_fox], axis=1)
    return _matmul(o, w_out.astype(BF16), seq=seq, out_dtype=F32, epilogue="residual",
                   mod=mod, x_res=x, gate_row=2, name="l0_out_proj")


def _mla_down_kernel(x_ref, gain_ref, mod_ref, wd_ref, qn_ref, kvn_ref, cos_ref, up_ref, dn_ref,
                     cq_ref, kin_ref, hb_ref):
    def store(r0, rows, h):
        hb_ref[pl.ds(r0, rows), :] = h.astype(BF16)
    _norm_mod_rows(x_ref, gain_ref, mod_ref, 0, 1, store)
    a = jnp.dot(hb_ref[...], wd_ref[...], preferred_element_type=F32)

    def rms(v, g):
        ms = jnp.mean(v * v, axis=-1, keepdims=True)
        return (v * lax.rsqrt(ms + NORM_EPS)) * g

    q_end = MLA_Q_LORA
    kv_end = MLA_Q_LORA + MLA_KV_LORA
    cq_ref[...] = rms(a[:, :q_end], qn_ref[...]).astype(BF16)
    kin_ref[:, :MLA_KV_LORA] = rms(a[:, q_end:kv_end], kvn_ref[...]).astype(BF16)
    kpe = _rope_group(a[:, kv_end:kv_end + LANES], cos_ref[...], up_ref[...], dn_ref[...],
                      MLA_ROPE_DIM // 2)
    kin_ref[:, MLA_KV_LORA:] = kpe.astype(BF16)


def _mla_down(x, gain, mod, w_down, q_norm, kv_norm, tables, seq):
    n, d = x.shape
    tm = _pick(seq, 512)
    tiles_per_seq = seq // tm
    wcols = w_down.shape[1]
    const = lambda i: (0, 0)
    return pl.pallas_call(
        _mla_down_kernel,
        out_shape=[jax.ShapeDtypeStruct((n, MLA_Q_LORA), BF16),
                   jax.ShapeDtypeStruct((n, MLA_KV_LORA + LANES), BF16)],
        grid=(n // tm,),
        in_specs=[pl.BlockSpec((tm, d), lambda i: (i, 0)),
                  pl.BlockSpec((1, d), const),
                  pl.BlockSpec((1, 6, d), lambda i: (i // tiles_per_seq, 0, 0)),
                  pl.BlockSpec((d, wcols), const),
                  pl.BlockSpec((1, MLA_Q_LORA), const),
                  pl.BlockSpec((1, MLA_KV_LORA), const)]
                 + [pl.BlockSpec((tm, LANES), lambda i: (i % tiles_per_seq, 0))] * 3,
        out_specs=[pl.BlockSpec((tm, MLA_Q_LORA), lambda i: (i, 0)),
                   pl.BlockSpec((tm, MLA_KV_LORA + LANES), lambda i: (i, 0))],
        scratch_shapes=[pltpu.VMEM((tm, d), BF16)],
        compiler_params=_cparams(("parallel",), 40),
        name="mla_down",
    )(x, gain.reshape(1, d), mod, w_down, q_norm.reshape(1, -1), kv_norm.reshape(1, -1), *tables)


def _mla_mixer(x, gain, mod, w_dq, q_norm, w_uq, w_dkv, kv_norm, w_ukv, w_out, batch, seq):
    n, d = x.shape
    dk = 2 * LANES
    pad_rope = LANES - MLA_ROPE_DIM
    qk_scale = (MLA_NOPE_DIM + MLA_ROPE_DIM) ** -0.5 * LOG2E
    tables = _rope_tables(seq, MLA_ROPE_DIM)
    w_down = jnp.concatenate(
        [w_dq, w_dkv, jnp.zeros((d, pad_rope), F32)], axis=1).astype(BF16)
    cq, kin = _mla_down(x, gain, mod, w_down, q_norm, kv_norm, tables, seq)

    w_q = (w_uq * qk_scale).reshape(MLA_Q_LORA, MLA_HEADS, MLA_NOPE_DIM + MLA_ROPE_DIM)
    w_q = jnp.pad(w_q, ((0, 0), (0, 0), (0, pad_rope))).reshape(MLA_Q_LORA, MLA_HEADS * dk)
    tn = 1024
    q = _matmul(cq, w_q.astype(BF16), seq=seq, epilogue="rot", rot_tables=tables,
                rot_groups=range(1, tn // LANES, 2), rot_half=MLA_ROPE_DIM // 2, tn=tn,
                name="mla_q_proj")

    w_kv = w_ukv.reshape(MLA_KV_LORA, MLA_HEADS, MLA_NOPE_DIM + MLA_V_DIM)
    w_knope = jnp.pad(w_kv[:, :, :MLA_NOPE_DIM], ((0, LANES), (0, 0), (0, LANES)))
    rope_place = jnp.eye(LANES, dk, k=LANES, dtype=F32) * (
        jnp.arange(LANES) < MLA_ROPE_DIM).astype(F32)[:, None]
    rope_rows = jnp.concatenate([jnp.zeros((MLA_KV_LORA, dk), F32), rope_place], axis=0)
    w_k = w_knope + rope_rows[:, None, :]
    w_v = jnp.pad(w_kv[:, :, MLA_NOPE_DIM:], ((0, LANES), (0, 0), (0, 0)))
    w_kv_all = jnp.concatenate([w_k.reshape(MLA_KV_LORA + LANES, MLA_HEADS * dk),
                                w_v.reshape(MLA_KV_LORA + LANES, MLA_HEADS * MLA_V_DIM)], axis=1)
    kv = _matmul(kin, w_kv_all.astype(BF16), seq=seq, tn=tn, name="mla_kv_proj")

    o = _attention(q, kv, kv, batch=batch, seq=seq, heads=MLA_HEADS, dk=dk, dv=MLA_V_DIM,
                   q_col=0, k_col=0, v_col=MLA_HEADS * dk // MLA_V_DIM, mode="plain",
                   name="mla_attn")
    return _matmul(o, w_out.astype(BF16), seq=seq, out_dtype=F32, epilogue="residual",
                   mod=mod, x_res=x, gate_row=2, name="l1_out_proj")


def kernel(x, c, ada_w, ada_b, norm_mix, norm_ffn, ab_w_in, ab_forget_bias, ab_w_out, mla_w_dq,
           mla_q_norm, mla_w_uq, mla_w_dkv, mla_kv_norm, mla_w_ukv, mla_w_out, router_w,
           router_bias, exp_w_gate, exp_w_up, exp_w_down, final_norm):
    batch, seq, d = x.shape
    depth = ada_w.shape[0]
    mod = _adaln(c, ada_w, ada_b)
    xs = x.reshape(batch * seq, d)
    for layer in range(depth):
        if layer % 2 == 0:
            xs = _moba_fox_mixer(xs, norm_mix[layer], mod[layer], ab_w_in, ab_forget_bias,
                                 ab_w_out, batch, seq)
        else:
            xs = _mla_mixer(xs, norm_mix[layer], mod[layer], mla_w_dq, mla_q_norm, mla_w_uq,
                            mla_w_dkv, mla_kv_norm, mla_w_ukv, mla_w_out, batch, seq)
        final_gain = final_norm if layer == depth - 1 else None
        xs = _moe_layer(xs, norm_ffn[layer], mod[layer], router_w, router_bias,
                        exp_w_gate, exp_w_up, exp_w_down, layer, seq, final_gain)
    return xs.reshape(batch, seq, d)
```

```python
import functools
import math

import jax
import jax.numpy as jnp
from jax import lax
from jax.experimental import pallas as pl
from jax.experimental.pallas import tpu as pltpu

F32 = jnp.float32
BF16 = jnp.bfloat16

HEAD_DIM = 128
MOBA_HEADS = 8
FOX_HEADS = 8
ROT_DIM = HEAD_DIM // 4
ROPE_THETA = 500000.0
MOBA_BLOCK = 256
MOBA_TOPK = 3
MLA_HEADS = 16
MLA_Q_LORA = 512
MLA_KV_LORA = 512
MLA_NOPE_DIM = 128
MLA_ROPE_DIM = 64
MLA_V_DIM = 128
N_EXPERTS = 16
N_GROUPS = 4
EXPERTS_PER_GROUP = N_EXPERTS // N_GROUPS
TOP_K = 2
NORM_EPS = 1e-6

LANES = 128
V7X_VMEM_BYTES = 64 * 1024 * 1024
MIB = 1024 * 1024
LOG2E = math.log2(math.e)

NT_DIMS = (((1,), (1,)), ((), ()))


def _cparams(semantics, vmem_mib):
    assert vmem_mib * MIB < V7X_VMEM_BYTES
    return pltpu.CompilerParams(dimension_semantics=semantics,
                                vmem_limit_bytes=vmem_mib * MIB)


def _pick(n, pref):
    t = min(pref, n)
    while n % t:
        t //= 2
    return t


def _adaln_kernel(c_ref, w_ref, b_ref, o_ref):
    c = c_ref[...]
    ca = (c * jax.nn.sigmoid(c)).astype(BF16)
    acc = jnp.dot(ca, w_ref[0].astype(BF16), preferred_element_type=F32)
    o_ref[0] = acc + b_ref[0]


def _adaln(c, ada_w, ada_b):
    depth, d, n6 = ada_w.shape
    b = c.shape[0]
    rows = 8
    c_pad = jnp.pad(c, ((0, rows - b), (0, 0)))
    tn = _pick(n6, 1024)
    out = pl.pallas_call(
        _adaln_kernel,
        out_shape=jax.ShapeDtypeStruct((depth, rows, n6), F32),
        grid=(depth, n6 // tn),
        in_specs=[
            pl.BlockSpec((rows, d), lambda l, j: (0, 0)),
            pl.BlockSpec((1, d, tn), lambda l, j: (l, 0, j)),
            pl.BlockSpec((1, 1, tn), lambda l, j: (l, 0, j)),
        ],
        out_specs=pl.BlockSpec((1, rows, tn), lambda l, j: (l, 0, j)),
        compiler_params=_cparams(("parallel", "parallel"), 40),
        name="adaln_mod",
    )(c_pad, ada_w, ada_b.reshape(depth, 1, n6))
    return out[:, :b, :].reshape(depth, b, 6, d)


NORM_ROWS = 256


def _norm_mod_rows(x_ref, gain_ref, mod_ref, shift_row, scale_row, store):
    tm = x_ref.shape[0]
    rows = min(NORM_ROWS, tm)
    gain = gain_ref[...]
    scale1p = 1.0 + mod_ref[0, scale_row:scale_row + 1, :]
    shift = mod_ref[0, shift_row:shift_row + 1, :]

    def body(c, carry):
        r0 = pl.multiple_of(c * rows, rows)
        x = x_ref[pl.ds(r0, rows), :]
        ms = jnp.mean(x * x, axis=-1, keepdims=True)
        y = x * lax.rsqrt(ms + NORM_EPS)
        h = (y * gain) * scale1p + shift
        store(r0, rows, h)
        return carry

    lax.fori_loop(0, tm // rows, body, 0)


def _rope_group(a, cos_t, sin_up, sin_dn, half):
    up = pltpu.roll(a, LANES - half, 1)
    dn = pltpu.roll(a, half, 1)
    return a * cos_t + up * sin_up + dn * sin_dn


def _rope_tables(seq, rot_dim):
    half = rot_dim // 2
    inv_freq = ROPE_THETA ** (-jnp.arange(half, dtype=F32) / half)
    ang = jnp.arange(seq, dtype=F32)[:, None] * inv_freq[None, :]
    cos, sin = jnp.cos(ang), jnp.sin(ang)
    zeros = jnp.zeros((seq, LANES - rot_dim), F32)
    zh = jnp.zeros((seq, half), F32)
    cos_t = jnp.concatenate([cos, cos, jnp.ones((seq, LANES - rot_dim), F32)], axis=1)
    sin_up = jnp.concatenate([-sin, zh, zeros], axis=1)
    sin_dn = jnp.concatenate([zh, sin, zeros], axis=1)
    return cos_t, sin_up, sin_dn


def _mm_kernel(*refs, prologue, epilogue, rot_groups, rot_tile_limit, rot_half,
               shift_row, scale_row, gate_row, has_side):
    it = iter(refs)
    a_ref = next(it)
    if prologue == "norm":
        gain_ref = next(it)
    if prologue == "norm" or epilogue == "residual":
        mod_ref = next(it)
    w_ref = next(it)
    if has_side:
        wside_ref = next(it)
    if epilogue == "rot":
        cos_ref, up_ref, dn_ref = next(it), next(it), next(it)
    if epilogue == "residual":
        x_ref = next(it)
    o_ref = next(it)
    if has_side:
        side_ref = next(it)
    if prologue == "norm":
        hb_ref = next(it)

    j = pl.program_id(1)

    if prologue == "norm":
        @pl.when(j == 0)
        def _():
            def store(r0, rows, h):
                hb_ref[pl.ds(r0, rows), :] = h.astype(BF16)
            _norm_mod_rows(a_ref, gain_ref, mod_ref, shift_row, scale_row, store)
            if has_side:
                side_ref[...] = jnp.dot(hb_ref[...], wside_ref[...],
                                        preferred_element_type=F32)
        lhs_ref = hb_ref
    else:
        lhs_ref = a_ref

    def product():
        return jnp.dot(lhs_ref[...], w_ref[...], preferred_element_type=F32)

    if epilogue == "rot":
        def rotated():
            acc = product()
            groups = []
            for g in range(acc.shape[1] // LANES):
                blk = acc[:, g * LANES:(g + 1) * LANES]
                if g in rot_groups:
                    blk = _rope_group(blk, cos_ref[...], up_ref[...], dn_ref[...], rot_half)
                groups.append(blk)
            return jnp.concatenate(groups, axis=1)

        if rot_tile_limit is None:
            o_ref[...] = rotated().astype(o_ref.dtype)
        else:
            @pl.when(j < rot_tile_limit)
            def _():
                o_ref[...] = rotated().astype(o_ref.dtype)

            @pl.when(j >= rot_tile_limit)
            def _():
                o_ref[...] = product().astype(o_ref.dtype)
    elif epilogue == "residual":
        gate = mod_ref[0, gate_row:gate_row + 1, :]
        o_ref[...] = x_ref[...] + gate * product()
    else:
        o_ref[...] = product().astype(o_ref.dtype)


def _matmul(a, w, *, seq, out_dtype=BF16, tm=1024, tn=512, prologue="none", gain=None,
            mod=None, shift_row=0, scale_row=1, epilogue="none", rot_tables=None,
            rot_groups=(), rot_tile_limit=None, rot_half=0, x_res=None, gate_row=2,
            w_side=None, name="matmul"):
    m, k = a.shape
    n = w.shape[1]
    tm = _pick(seq, tm)
    tn = _pick(n, tn)
    tiles_per_seq = seq // tm
    has_side = w_side is not None
    in_specs = [pl.BlockSpec((tm, k), lambda i, j: (i, 0))]
    args = [a]
    if prologue == "norm":
        in_specs.append(pl.BlockSpec((1, k), lambda i, j: (0, 0)))
        args.append(gain.reshape(1, k))
    if prologue == "norm":
        in_specs.append(pl.BlockSpec((1, 6, k), lambda i, j: (i // tiles_per_seq, 0, 0)))
        args.append(mod)
    elif epilogue == "residual":
        in_specs.append(pl.BlockSpec((1, 6, tn), lambda i, j: (i // tiles_per_seq, 0, j)))
        args.append(mod)
    in_specs.append(pl.BlockSpec((k, tn), lambda i, j: (0, j)))
    args.append(w)
    if has_side:
        in_specs.append(pl.BlockSpec((k, LANES), lambda i, j: (0, 0)))
        args.append(w_side)
    if epilogue == "rot":
        for t in rot_tables:
            in_specs.append(pl.BlockSpec((tm, LANES), lambda i, j: (i % tiles_per_seq, 0)))
            args.append(t)
    if epilogue == "residual":
        in_specs.append(pl.BlockSpec((tm, tn), lambda i, j: (i, j)))
        args.append(x_res)
    out_shape = [jax.ShapeDtypeStruct((m, n), out_dtype)]
    out_specs = [pl.BlockSpec((tm, tn), lambda i, j: (i, j))]
    if has_side:
        out_shape.append(jax.ShapeDtypeStruct((m, LANES), F32))
        out_specs.append(pl.BlockSpec((tm, LANES), lambda i, j: (i, 0)))
    scratch = [pltpu.VMEM((tm, k), BF16)] if prologue == "norm" else []
    kernel = functools.partial(
        _mm_kernel, prologue=prologue, epilogue=epilogue, rot_groups=tuple(rot_groups),
        rot_tile_limit=rot_tile_limit, rot_half=rot_half, shift_row=shift_row,
        scale_row=scale_row, gate_row=gate_row, has_side=has_side)
    outs = pl.pallas_call(
        kernel,
        out_shape=out_shape,
        grid=(m // tm, n // tn),
        in_specs=in_specs,
        out_specs=out_specs,
        scratch_shapes=scratch,
        compiler_params=_cparams(("parallel", "arbitrary"), 48),
        name=name,
    )(*args)
    return outs if has_side else outs[0]


def _fox_prep_kernel(flog_ref, bias_ref, qx_ref, kx_ref, carry_ref):
    tc = flog_ref.shape[0]

    @pl.when(pl.program_id(1) == 0)
    def _():
        carry_ref[...] = jnp.zeros_like(carry_ref)

    z = flog_ref[...] + bias_ref[...]
    logf = jnp.minimum(z, 0.0) - jnp.log1p(jnp.exp(-jnp.abs(z)))
    row = lax.broadcasted_iota(jnp.int32, (tc, tc), 0)
    col = lax.broadcasted_iota(jnp.int32, (tc, tc), 1)
    tri = jnp.where(col <= row, 1.0, 0.0).astype(F32)
    csum = jnp.dot(tri, logf, preferred_element_type=F32,
                   precision=lax.Precision.HIGHEST) + carry_ref[...]
    carry_ref[...] = csum[tc - 1:tc, :]
    csum = csum * LOG2E
    lane = lax.broadcasted_iota(jnp.int32, (tc, LANES), 1)
    one = jnp.ones((tc, LANES), F32)
    zero = jnp.zeros((tc, LANES), F32)
    for h in range(FOX_HEADS):
        colv = jnp.broadcast_to(csum[:, h:h + 1], (tc, LANES))
        hi = colv.astype(BF16).astype(F32)
        r1 = colv - hi
        mid = r1.astype(BF16).astype(F32)
        lo = r1 - mid
        qv = jnp.where(lane == 0, hi, jnp.where(lane == 1, mid, jnp.where(
            lane == 2, lo, jnp.where(lane < 6, one, zero))))
        kv = jnp.where(lane < 3, one, jnp.where(lane == 3, -hi, jnp.where(
            lane == 4, -mid, jnp.where(lane == 5, -lo, zero))))
        qx_ref[:, h * LANES:(h + 1) * LANES] = qv.astype(BF16)
        kx_ref[:, h * LANES:(h + 1) * LANES] = kv.astype(BF16)


def _fox_prep(flog, forget_bias, batch, seq):
    n = flog.shape[0]
    tc = _pick(seq, 256)
    bias = jnp.pad(forget_bias, (0, LANES - FOX_HEADS)).reshape(1, LANES)
    width = FOX_HEADS * LANES
    steps = seq // tc
    return pl.pallas_call(
        _fox_prep_kernel,
        out_shape=[jax.ShapeDtypeStruct((n, width), BF16)] * 2,
        grid=(batch, steps),
        in_specs=[pl.BlockSpec((tc, LANES), lambda b, s: (b * steps + s, 0)),
                  pl.BlockSpec((1, LANES), lambda b, s: (0, 0))],
        out_specs=[pl.BlockSpec((tc, width), lambda b, s: (b * steps + s, 0))] * 2,
        scratch_shapes=[pltpu.VMEM((1, LANES), F32)],
        compiler_params=_cparams(("parallel", "arbitrary"), 32),
        name="fox_prep",
    )(flog, bias)


ATTN_TILE = 512
ATTN_HEADS_PER_STEP = 8


M_INIT = -1e30


def _attn_kernel(qt_ref, kt_ref, *refs, mode, tq, dk, dv):
    it = iter(refs)
    q_ref, k_ref, v_ref = next(it), next(it), next(it)
    if mode == "fox":
        qx_ref, kx_ref = next(it), next(it)
    if mode == "moba":
        kmean_ref = next(it)
    o_ref = next(it)
    m_ref, l_ref, acc_ref = next(it), next(it), next(it)
    if mode == "moba":
        sel_ref = next(it)

    p_id = pl.program_id(2)
    i, j = qt_ref[p_id], kt_ref[p_id]
    neg_inf = jnp.float32(-jnp.inf)
    heads = q_ref.shape[1] // dk
    blocks_per_tile = tq // MOBA_BLOCK
    block_shift = MOBA_BLOCK.bit_length() - 1

    def q_of(h):
        q = q_ref[:, h * dk:(h + 1) * dk]
        if mode == "fox":
            q = jnp.concatenate([q, qx_ref[:, h * LANES:(h + 1) * LANES]], axis=1)
        return q

    def scores(h):
        k = k_ref[:, h * dk:(h + 1) * dk]
        if mode == "fox":
            k = jnp.concatenate([k, kx_ref[:, h * LANES:(h + 1) * LANES]], axis=1)
        return lax.dot_general(q_of(h), k, NT_DIMS, preferred_element_type=F32)

    def block_hits(h):
        lane = lax.broadcasted_iota(jnp.int32, (tq, LANES), 1)
        sel = sel_ref[h]
        return [jnp.broadcast_to(
            jnp.max(jnp.where(lane == j * blocks_per_tile + c, sel, 0.0), axis=1, keepdims=True),
            (tq, LANES)) > 0.0 for c in range(blocks_per_tile)]

    def masked(s, allowed_of_group):
        groups = [jnp.where(allowed_of_group(g), s[:, g * LANES:(g + 1) * LANES], neg_inf)
                  for g in range(tq // LANES)]
        return jnp.concatenate(groups, axis=1)

    def update(h, s):
        m_old = m_ref[h]
        row_max = jnp.broadcast_to(jnp.max(s, axis=1, keepdims=True), m_old.shape)
        m_new = jnp.maximum(m_old, row_max)
        alpha = jnp.exp2(m_old - m_new)
        p = jnp.exp2(s - jnp.concatenate([m_new] * (s.shape[1] // LANES), axis=1))
        row_sum = jnp.broadcast_to(jnp.sum(p, axis=1, keepdims=True), m_old.shape)
        l_ref[h] = alpha * l_ref[h] + row_sum
        acc_ref[h] = alpha * acc_ref[h] + jnp.dot(
            p.astype(BF16), v_ref[:, h * dv:(h + 1) * dv], preferred_element_type=F32)
        m_ref[h] = m_new

    @pl.when(j == 0)
    def _():
        m_ref[...] = jnp.full_like(m_ref, M_INIT)
        l_ref[...] = jnp.zeros_like(l_ref)
        acc_ref[...] = jnp.zeros_like(acc_ref)
        if mode == "moba":
            nb = kmean_ref.shape[0]
            blk = lax.broadcasted_iota(jnp.int32, (nb, tq), 0)
            blk_f = blk.astype(F32)
            query_block = jnp.right_shift(lax.broadcasted_iota(jnp.int32, (nb, tq), 1),
                                          block_shift)
            past = blk < i * blocks_per_tile + query_block
            for h in range(heads):
                gate = lax.dot_general(kmean_ref[:, h * dk:(h + 1) * dk], q_of(h).astype(F32),
                                       NT_DIMS, preferred_element_type=F32,
                                       precision=lax.Precision.HIGHEST)
                g = jnp.where(past, gate, neg_inf)
                chosen = jnp.zeros(gate.shape, F32)
                for _ in range(min(MOBA_TOPK, nb - 1)):
                    mx = jnp.max(g, axis=0, keepdims=True)
                    first = jnp.min(jnp.where(g == mx, blk_f, float(LANES)), axis=0,
                                    keepdims=True)
                    pick = blk_f == first
                    chosen = jnp.where(jnp.logical_and(pick, past), 1.0, chosen)
                    g = jnp.where(pick, neg_inf, g)
                chosen = jnp.concatenate([chosen, jnp.zeros((LANES - nb, tq), F32)], axis=0)
                sel_ref[h] = chosen.T

    groups_per_block = MOBA_BLOCK // LANES

    @pl.when(j < i)
    def _():
        for h in range(heads):
            s = scores(h)
            if mode == "moba":
                hits = block_hits(h)
                s = masked(s, lambda g: hits[g // groups_per_block])
            update(h, s)

    @pl.when(j == i)
    def _():
        row = lax.broadcasted_iota(jnp.int32, (tq, LANES), 0)
        lane = lax.broadcasted_iota(jnp.int32, (tq, LANES), 1)
        row_blk = jnp.right_shift(row, block_shift)
        for h in range(heads):
            s = scores(h)
            if mode == "moba":
                hits = block_hits(h)

                def allowed(g):
                    col_blk = g // groups_per_block
                    own = jnp.logical_and(row_blk == col_blk, lane + g * LANES <= row)
                    return jnp.logical_or(own, jnp.logical_and(row_blk > col_blk, hits[col_blk]))
            else:
                def allowed(g):
                    return lane + g * LANES <= row
            update(h, masked(s, allowed))
            o_ref[:, h * dv:(h + 1) * dv] = (acc_ref[h] / l_ref[h]).astype(o_ref.dtype)


def _attention(q_arr, k_arr, v_arr, *, batch, seq, heads, dk, dv, q_col, k_col, v_col, mode,
               qx=None, kx=None, kmean=None, name="attn"):
    n = batch * seq
    tq = _pick(seq, ATTN_TILE)
    g = ATTN_HEADS_PER_STEP
    assert tq % MOBA_BLOCK == 0 and heads % g == 0
    assert q_col % g == 0 and k_col % g == 0 and v_col % g == 0
    nq = seq // tq
    pairs = [(i, j) for i in range(nq) for j in range(i + 1)]
    q_tile = jnp.array([p[0] for p in pairs], jnp.int32)
    k_tile = jnp.array([p[1] for p in pairs], jnp.int32)

    def q_rows(b, h, p, qt, kt):
        return b * nq + qt[p]

    def k_rows(b, h, p, qt, kt):
        return b * nq + kt[p]

    in_specs = [
        pl.BlockSpec((tq, g * dk), lambda b, h, p, qt, kt: (q_rows(b, h, p, qt, kt), q_col // g + h)),
        pl.BlockSpec((tq, g * dk), lambda b, h, p, qt, kt: (k_rows(b, h, p, qt, kt), k_col // g + h)),
        pl.BlockSpec((tq, g * dv), lambda b, h, p, qt, kt: (k_rows(b, h, p, qt, kt), v_col // g + h)),
    ]
    args = [q_arr, k_arr, v_arr]
    if mode == "fox":
        in_specs += [
            pl.BlockSpec((tq, g * LANES), lambda b, h, p, qt, kt: (q_rows(b, h, p, qt, kt), h)),
            pl.BlockSpec((tq, g * LANES), lambda b, h, p, qt, kt: (k_rows(b, h, p, qt, kt), h))]
        args += [qx, kx]
    if mode == "moba":
        nb = seq // MOBA_BLOCK
        in_specs.append(pl.BlockSpec((nb, g * dk), lambda b, h, p, qt, kt: (b, h)))
        args.append(kmean)
    assert dv == LANES
    scratch = [pltpu.VMEM((g, tq, LANES), F32), pltpu.VMEM((g, tq, LANES), F32),
               pltpu.VMEM((g, tq, dv), F32)]
    if mode == "moba":
        scratch.append(pltpu.VMEM((g, tq, LANES), F32))
    return pl.pallas_call(
        functools.partial(_attn_kernel, mode=mode, tq=tq, dk=dk, dv=dv),
        out_shape=jax.ShapeDtypeStruct((n, heads * dv), BF16),
        grid_spec=pltpu.PrefetchScalarGridSpec(
            num_scalar_prefetch=2,
            grid=(batch, heads // g, len(pairs)),
            in_specs=in_specs,
            out_specs=pl.BlockSpec((tq, g * dv),
                                   lambda b, h, p, qt, kt: (q_rows(b, h, p, qt, kt), h)),
            scratch_shapes=scratch,
        ),
        compiler_params=_cparams(("parallel", "parallel", "arbitrary"), 40),
        name=name,
    )(q_tile, k_tile, *args)


def _block_means_kernel(k_ref, o_ref):
    for b in range(o_ref.shape[0]):
        kb = k_ref[b * MOBA_BLOCK:(b + 1) * MOBA_BLOCK, :].astype(F32)
        o_ref[b:b + 1, :] = jnp.mean(kb, axis=0, keepdims=True)


def _block_means(arr, col0, cols):
    n = arr.shape[0]
    blocks = 8
    rows = blocks * MOBA_BLOCK
    assert n % rows == 0 and col0 % cols == 0
    return pl.pallas_call(
        _block_means_kernel,
        out_shape=jax.ShapeDtypeStruct((n // MOBA_BLOCK, cols), F32),
        grid=(n // rows,),
        in_specs=[pl.BlockSpec((rows, cols), lambda i: (i, col0 // cols))],
        out_specs=pl.BlockSpec((blocks, cols), lambda i: (i, 0)),
        compiler_params=_cparams(("parallel",), 32),
        name="moba_block_means",
    )(arr)


def _top2_of4(vals):
    a, b, c, d = vals
    hi1, lo1 = jnp.maximum(a, b), jnp.minimum(a, b)
    hi2, lo2 = jnp.maximum(c, d), jnp.minimum(c, d)
    return jnp.maximum(hi1, hi2) + jnp.maximum(jnp.minimum(hi1, hi2), jnp.maximum(lo1, lo2))


def _route_kernel(x_ref, gain_ref, mod_ref, rwt_ref, rbias_ref, h_ref, idx_ref, wgt_ref):
    def store(r0, rows, h):
        h_ref[pl.ds(r0, rows), :] = h
    _norm_mod_rows(x_ref, gain_ref, mod_ref, 3, 4, store)

    logits = lax.dot_general(rwt_ref[...], h_ref[...], NT_DIMS, preferred_element_type=F32,
                             precision=lax.Precision.HIGHEST)
    score = jax.nn.sigmoid(logits)
    biased = score + rbias_ref[...]
    rows_b = [biased[e:e + 1, :] for e in range(N_EXPERTS)]
    rows_s = [score[e:e + 1, :] for e in range(N_EXPERTS)]
    grp = [_top2_of4(rows_b[g * EXPERTS_PER_GROUP:(g + 1) * EXPERTS_PER_GROUP])
           for g in range(N_GROUPS)]
    best, g_sel = grp[0], jnp.zeros_like(grp[0], dtype=jnp.int32)
    for g in range(1, N_GROUPS):
        better = grp[g] > best
        best = jnp.where(better, grp[g], best)
        g_sel = jnp.where(better, g, g_sel)

    def in_group(rows, r):
        out = rows[r]
        for g in range(1, N_GROUPS):
            out = jnp.where(g_sel == g, rows[g * EXPERTS_PER_GROUP + r], out)
        return out

    cb = [in_group(rows_b, r) for r in range(EXPERTS_PER_GROUP)]
    cs = [in_group(rows_s, r) for r in range(EXPERTS_PER_GROUP)]
    neg_inf = jnp.float32(-jnp.inf)
    picks = []
    for _ in range(TOP_K):
        bv, bi, bs = cb[0], jnp.zeros_like(g_sel), cs[0]
        for r in range(1, EXPERTS_PER_GROUP):
            better = cb[r] > bv
            bv = jnp.where(better, cb[r], bv)
            bi = jnp.where(better, r, bi)
            bs = jnp.where(better, cs[r], bs)
        picks.append((bi, bs))
        cb = [jnp.where(bi == r, neg_inf, cb[r]) for r in range(EXPERTS_PER_GROUP)]
    (i0, s0), (i1, s1) = picks
    total = s0 + s1
    idx_ref[0:1, :] = g_sel * EXPERTS_PER_GROUP + i0
    idx_ref[1:2, :] = g_sel * EXPERTS_PER_GROUP + i1
    wgt_ref[0:1, :] = s0 / total
    wgt_ref[1:2, :] = s1 / total


def _route(x, gain, mod, router_w, router_bias, seq):
    n, d = x.shape
    tm = _pick(seq, 512)
    tiles_per_seq = seq // tm
    return pl.pallas_call(
        _route_kernel,
        out_shape=[jax.ShapeDtypeStruct((n, d), F32),
                   jax.ShapeDtypeStruct((TOP_K, n), jnp.int32),
                   jax.ShapeDtypeStruct((TOP_K, n), F32)],
        grid=(n // tm,),
        in_specs=[pl.BlockSpec((tm, d), lambda i: (i, 0)),
                  pl.BlockSpec((1, d), lambda i: (0, 0)),
                  pl.BlockSpec((1, 6, d), lambda i: (i // tiles_per_seq, 0, 0)),
                  pl.BlockSpec((N_EXPERTS, d), lambda i: (0, 0)),
                  pl.BlockSpec((N_EXPERTS, 1), lambda i: (0, 0))],
        out_specs=[pl.BlockSpec((tm, d), lambda i: (i, 0)),
                   pl.BlockSpec((TOP_K, tm), lambda i: (0, i)),
                   pl.BlockSpec((TOP_K, tm), lambda i: (0, i))],
        compiler_params=_cparams(("parallel",), 32),
        name="moe_route",
    )(x, gain.reshape(1, d), mod, router_w.T, router_bias.reshape(N_EXPERTS, 1))


GMM_ROWS = 512
GMM_UP_COLS = 128
GMM_DOWN_COLS = 512
GMM_ISSUE_STEPS = 8
PACK_ROWS = 256
DMA_UNROLL = 8


def _pack_kernel(wg_ref, wu_ref, wd_ref, wgu_ref, wdb_ref):
    n_up = wgu_ref.shape[1]
    for f in range(n_up):
        cols = slice(f * GMM_UP_COLS, (f + 1) * GMM_UP_COLS)
        wgu_ref[0, f, :, :GMM_UP_COLS] = wg_ref[0, 0, :, cols].astype(BF16)
        wgu_ref[0, f, :, GMM_UP_COLS:] = wu_ref[0, 0, :, cols].astype(BF16)
    wdb_ref[0] = wd_ref[0, 0].astype(BF16)


def _pack_expert_weights(w_gate, w_up, w_down, layer):
    _, n_exp, d, d_exp = w_gate.shape
    n_up = d_exp // GMM_UP_COLS
    steps = d // PACK_ROWS
    down_rows = d_exp // steps
    assert d_exp % GMM_UP_COLS == 0 and d % PACK_ROWS == 0 and down_rows % 16 == 0
    up_spec = pl.BlockSpec((1, 1, PACK_ROWS, d_exp), lambda e, k: (layer, e, k, 0))
    return pl.pallas_call(
        _pack_kernel,
        out_shape=[jax.ShapeDtypeStruct((n_exp, n_up, d, 2 * GMM_UP_COLS), BF16),
                   jax.ShapeDtypeStruct((n_exp, d_exp, d), BF16)],
        grid=(n_exp, steps),
        in_specs=[up_spec, up_spec,
                  pl.BlockSpec((1, 1, down_rows, d), lambda e, k: (layer, e, k, 0))],
        out_specs=[pl.BlockSpec((1, n_up, PACK_ROWS, 2 * GMM_UP_COLS), lambda e, k: (e, 0, k, 0)),
                   pl.BlockSpec((1, down_rows, d), lambda e, k: (e, k, 0))],
        compiler_params=_cparams(("parallel", "parallel"), 32),
        name="moe_pack_weights",
    )(w_gate, w_up, w_down)


def _ffn_kernel(eid_ref, nvalid_ref, dst_ref, h_hbm, wgu_ref, wd_ref, y_hbm,
                xbuf_ref, hb_ref, act_ref, ystage_ref, gsem, ssem):
    t = pl.program_id(0)
    rows = hb_ref.shape[0]
    n_up = act_ref.shape[0]
    n_tokens = h_hbm.shape[0]
    per_chunk = rows // GMM_ISSUE_STEPS
    nv = nvalid_ref[0]
    valid = t < nv

    def dst_of(tile, r):
        return dst_ref[(tile + 1) * rows + r]

    def token_of(dst_row):
        if n_tokens & (n_tokens - 1) == 0:
            return jnp.bitwise_and(dst_row, n_tokens - 1)
        return lax.rem(dst_row, n_tokens)

    def gather_row(r, src_row):
        return pltpu.make_async_copy(h_hbm.at[pl.ds(src_row, 1), :],
                                     xbuf_ref.at[pl.ds(r, 1), :], gsem)

    def scatter_row(r, dst_row):
        return pltpu.make_async_copy(ystage_ref.at[pl.ds(r, 1), :],
                                     y_hbm.at[pl.ds(dst_row, 1), :], ssem)

    def for_rows(fn):
        def body(r, carry):
            fn(r)
            return carry
        lax.fori_loop(0, rows, body, 0, unroll=DMA_UNROLL)

    @pl.when(t == 0)
    def _():
        ystage_ref[...] = jnp.zeros_like(ystage_ref)
        for_rows(lambda r: gather_row(r, token_of(dst_of(0, r))).start())

    @pl.when(t <= nv)
    def _():
        for_rows(lambda r: gather_row(r, 0).wait())

    @pl.when(t == nv)
    def _():
        for_rows(lambda r: scatter_row(r, dst_of(t - 1, r)).start())
        for_rows(lambda r: scatter_row(r, 0).wait())

    @pl.when(valid)
    def _():
        hb_ref[...] = xbuf_ref[...].astype(BF16)
        for f in range(n_up):
            if f < GMM_ISSUE_STEPS:
                for r in range(f * per_chunk, (f + 1) * per_chunk):
                    gather_row(r, token_of(dst_of(t + 1, r))).start()
                    scatter_row(r, dst_of(t - 1, r)).start()
            gu = jnp.dot(hb_ref[...], wgu_ref[0, f], preferred_element_type=F32)
            g, u = gu[:, :GMM_UP_COLS], gu[:, GMM_UP_COLS:]
            act_ref[f] = ((g * jax.nn.sigmoid(g)) * u).astype(BF16)
        for_rows(lambda r: scatter_row(r, 0).wait())
        act = jnp.concatenate([act_ref[k] for k in range(n_up)], axis=1)
        ystage_ref[...] = jnp.dot(act, wd_ref[0], preferred_element_type=F32)


def _expert_ffn(eid, nvalid, dst, h, wgu, wdb, n_slots):
    n_tiles = dst.shape[0] // GMM_ROWS - 1
    d = h.shape[1]
    n_exp, n_up, _, up_cols = wgu.shape
    d_exp = wdb.shape[1]
    assert GMM_ROWS % GMM_ISSUE_STEPS == 0 and n_up >= GMM_ISSUE_STEPS
    return pl.pallas_call(
        _ffn_kernel,
        out_shape=jax.ShapeDtypeStruct((n_slots + GMM_ROWS, d), F32),
        grid_spec=pltpu.PrefetchScalarGridSpec(
            num_scalar_prefetch=3,
            grid=(n_tiles,),
            in_specs=[
                pl.BlockSpec(memory_space=pl.ANY),
                pl.BlockSpec((1, n_up, d, up_cols), lambda t, e, nv, ds: (e[t], 0, 0, 0)),
                pl.BlockSpec((1, d_exp, d), lambda t, e, nv, ds: (e[t], 0, 0)),
            ],
            out_specs=pl.BlockSpec(memory_space=pl.ANY),
            scratch_shapes=[pltpu.VMEM((GMM_ROWS, d), F32),
                            pltpu.VMEM((GMM_ROWS, d), BF16),
                            pltpu.VMEM((n_up, GMM_ROWS, GMM_UP_COLS), BF16),
                            pltpu.VMEM((GMM_ROWS, d), F32),
                            pltpu.SemaphoreType.DMA,
                            pltpu.SemaphoreType.DMA],
        ),
        compiler_params=_cparams(("arbitrary",), 56),
        name="moe_ffn",
    )(eid, nvalid, dst, h, wgu, wdb)


def _combine_kernel(x_ref, y0_ref, y1_ref, w_ref, mod_ref, *rest, final):
    if final:
        gain_ref, o_ref = rest
    else:
        (o_ref,) = rest
    rows = min(NORM_ROWS, x_ref.shape[0])
    gate = mod_ref[0, 5:6, :]

    def body(c, carry):
        r0 = pl.multiple_of(c * rows, rows)
        w = w_ref[pl.ds(r0, rows), :]
        y = (w[:, 0:1] * y0_ref[pl.ds(r0, rows), :]
             + w[:, 1:2] * y1_ref[pl.ds(r0, rows), :])
        out = x_ref[pl.ds(r0, rows), :] + gate * y
        if final:
            ms = jnp.mean(out * out, axis=-1, keepdims=True)
            out = (out * lax.rsqrt(ms + NORM_EPS)) * gain_ref[...]
        o_ref[pl.ds(r0, rows), :] = out
        return carry

    lax.fori_loop(0, x_ref.shape[0] // rows, body, 0)


def _combine(x, y_slots, wgt, mod, seq, final_gain=None):
    n, d = x.shape
    tm = _pick(seq, 512)
    tiles_per_seq = seq // tm
    tiles = n // tm
    final = final_gain is not None
    in_specs = [pl.BlockSpec((tm, d), lambda i: (i, 0)),
                pl.BlockSpec((tm, d), lambda i: (i, 0)),
                pl.BlockSpec((tm, d), lambda i: (tiles + i, 0)),
                pl.BlockSpec((tm, TOP_K), lambda i: (i, 0)),
                pl.BlockSpec((1, 6, d), lambda i: (i // tiles_per_seq, 0, 0))]
    args = [x, y_slots, y_slots, wgt, mod]
    if final:
        in_specs.append(pl.BlockSpec((1, d), lambda i: (0, 0)))
        args.append(final_gain.reshape(1, d))
    return pl.pallas_call(
        functools.partial(_combine_kernel, final=final),
        out_shape=jax.ShapeDtypeStruct((n, d), F32),
        grid=(tiles,),
        in_specs=in_specs,
        out_specs=pl.BlockSpec((tm, d), lambda i: (i, 0)),
        compiler_params=_cparams(("parallel",), 48),
        name="moe_combine",
    )(*args)


def _dispatch_plan(idx):
    n = idx.shape[1]
    slots = TOP_K * n
    cap = slots + N_EXPERTS * GMM_ROWS
    expert = idx.reshape(slots)
    onehot = (expert[:, None] == jnp.arange(N_EXPERTS)[None, :]).astype(jnp.int32)
    csum = jnp.cumsum(onehot, axis=0)
    rank = jnp.sum(csum * onehot, axis=1) - 1
    counts = csum[-1]
    padded = -(-counts // GMM_ROWS) * GMM_ROWS
    ends = jnp.cumsum(padded)
    pos = (ends - padded)[expert] + rank
    slot_ids = jnp.arange(slots, dtype=jnp.int32)
    dummy = slots + jnp.arange(cap + GMM_ROWS, dtype=jnp.int32) % GMM_ROWS
    dst = dummy.at[pos + GMM_ROWS].set(slot_ids)
    nvalid = (ends[-1] // GMM_ROWS).astype(jnp.int32)
    tile_start = jnp.arange(cap // GMM_ROWS, dtype=jnp.int32) * GMM_ROWS
    tile_start = jnp.minimum(tile_start, ends[-1] - GMM_ROWS)
    eid = jnp.sum(tile_start[:, None] >= ends[None, :], axis=1).astype(jnp.int32)
    return dst, eid, nvalid.reshape(1)


def _moe_layer(x, gain, mod, router_w, router_bias, w_gate, w_up, w_down, layer, seq,
               final_gain):
    wgu, wdb = _pack_expert_weights(w_gate, w_up, w_down, layer)
    h, idx, wgt = _route(x, gain, mod, router_w, router_bias, seq)
    dst, eid, nvalid = _dispatch_plan(idx)
    y_slots = _expert_ffn(eid, nvalid, dst, h, wgu, wdb, TOP_K * x.shape[0])
    return _combine(x, y_slots, wgt.T, mod, seq, final_gain)


def _moba_fox_mixer(x, gain, mod, w_in, forget_bias, w_out, batch, seq):
    n, d = x.shape
    mix = (MOBA_HEADS + FOX_HEADS) * HEAD_DIM
    qk_scale = HEAD_DIM ** -0.5 * LOG2E
    s1 = MOBA_HEADS * HEAD_DIM
    s2 = FOX_HEADS * HEAD_DIM
    col_scale = jnp.ones((3 * mix,), F32)
    col_scale = col_scale.at[:s1].set(qk_scale).at[3 * s1:3 * s1 + s2].set(qk_scale)
    w_main = (w_in[:, :3 * mix] * col_scale[None, :]).astype(BF16)
    w_forget = jnp.pad(w_in[:, 3 * mix:], ((0, 0), (0, LANES - FOX_HEADS))).astype(BF16)
    tn = 1024
    proj, flog = _matmul(
        x, w_main, seq=seq, prologue="norm", gain=gain, mod=mod, shift_row=0, scale_row=1,
        epilogue="rot", rot_tables=_rope_tables(seq, ROT_DIM), rot_groups=range(tn // LANES),
        rot_tile_limit=2 * s1 // tn, rot_half=ROT_DIM // 2, w_side=w_forget, tn=tn,
        name="l0_in_proj")
    qx, kx = _fox_prep(flog, forget_bias, batch, seq)
    o_moba = _attention(proj, proj, proj, batch=batch, seq=seq, heads=MOBA_HEADS,
                        dk=HEAD_DIM, dv=HEAD_DIM, q_col=0, k_col=MOBA_HEADS,
                        v_col=2 * MOBA_HEADS, mode="moba", kmean=_block_means(proj, s1, s1),
                        name="moba_attn")
    base = 3 * MOBA_HEADS
    o_fox = _attention(proj, proj, proj, batch=batch, seq=seq, heads=FOX_HEADS,
                       dk=HEAD_DIM, dv=HEAD_DIM, q_col=base, k_col=base + FOX_HEADS,
                       v_col=base + 2 * FOX_HEADS, mode="fox", qx=qx, kx=kx,
                       name="fox_attn")
    o = jnp.concatenate([o_moba, o_fox], axis=1)
    return _matmul(o, w_out.astype(BF16), seq=seq, out_dtype=F32, epilogue="residual",
                   mod=mod, x_res=x, gate_row=2, name="l0_out_proj")


def _mla_down_kernel(x_ref, gain_ref, mod_ref, wd_ref, qn_ref, kvn_ref, cos_ref, up_ref, dn_ref,
                     cq_ref, kin_ref, hb_ref):
    def store(r0, rows, h):
        hb_ref[pl.ds(r0, rows), :] = h.astype(BF16)
    _norm_mod_rows(x_ref, gain_ref, mod_ref, 0, 1, store)
    a = jnp.dot(hb_ref[...], wd_ref[...], preferred_element_type=F32)

    def rms(v, g):
        ms = jnp.mean(v * v, axis=-1, keepdims=True)
        return (v * lax.rsqrt(ms + NORM_EPS)) * g

    q_end = MLA_Q_LORA
    kv_end = MLA_Q_LORA + MLA_KV_LORA
    cq_ref[...] = rms(a[:, :q_end], qn_ref[...]).astype(BF16)
    kin_ref[:, :MLA_KV_LORA] = rms(a[:, q_end:kv_end], kvn_ref[...]).astype(BF16)
    kpe = _rope_group(a[:, kv_end:kv_end + LANES], cos_ref[...], up_ref[...], dn_ref[...],
                      MLA_ROPE_DIM // 2)
    kin_ref[:, MLA_KV_LORA:] = kpe.astype(BF16)


def _mla_down(x, gain, mod, w_down, q_norm, kv_norm, tables, seq):
    n, d = x.shape
    tm = _pick(seq, 512)
    tiles_per_seq = seq // tm
    wcols = w_down.shape[1]
    const = lambda i: (0, 0)
    return pl.pallas_call(
        _mla_down_kernel,
        out_shape=[jax.ShapeDtypeStruct((n, MLA_Q_LORA), BF16),
                   jax.ShapeDtypeStruct((n, MLA_KV_LORA + LANES), BF16)],
        grid=(n // tm,),
        in_specs=[pl.BlockSpec((tm, d), lambda i: (i, 0)),
                  pl.BlockSpec((1, d), const),
                  pl.BlockSpec((1, 6, d), lambda i: (i // tiles_per_seq, 0, 0)),
                  pl.BlockSpec((d, wcols), const),
                  pl.BlockSpec((1, MLA_Q_LORA), const),
                  pl.BlockSpec((1, MLA_KV_LORA), const)]
                 + [pl.BlockSpec((tm, LANES), lambda i: (i % tiles_per_seq, 0))] * 3,
        out_specs=[pl.BlockSpec((tm, MLA_Q_LORA), lambda i: (i, 0)),
                   pl.BlockSpec((tm, MLA_KV_LORA + LANES), lambda i: (i, 0))],
        scratch_shapes=[pltpu.VMEM((tm, d), BF16)],
        compiler_params=_cparams(("parallel",), 40),
        name="mla_down",
    )(x, gain.reshape(1, d), mod, w_down, q_norm.reshape(1, -1), kv_norm.reshape(1, -1), *tables)


def _mla_mixer(x, gain, mod, w_dq, q_norm, w_uq, w_dkv, kv_norm, w_ukv, w_out, batch, seq):
    n, d = x.shape
    dk = 2 * LANES
    pad_rope = LANES - MLA_ROPE_DIM
    qk_scale = (MLA_NOPE_DIM + MLA_ROPE_DIM) ** -0.5 * LOG2E
    tables = _rope_tables(seq, MLA_ROPE_DIM)
    w_down = jnp.concatenate(
        [w_dq, w_dkv, jnp.zeros((d, pad_rope), F32)], axis=1).astype(BF16)
    cq, kin = _mla_down(x, gain, mod, w_down, q_norm, kv_norm, tables, seq)

    w_q = (w_uq * qk_scale).reshape(MLA_Q_LORA, MLA_HEADS, MLA_NOPE_DIM + MLA_ROPE_DIM)
    w_q = jnp.pad(w_q, ((0, 0), (0, 0), (0, pad_rope))).reshape(MLA_Q_LORA, MLA_HEADS * dk)
    tn = 1024
    q = _matmul(cq, w_q.astype(BF16), seq=seq, epilogue="rot", rot_tables=tables,
                rot_groups=range(1, tn // LANES, 2), rot_half=MLA_ROPE_DIM // 2, tn=tn,
                name="mla_q_proj")

    w_kv = w_ukv.reshape(MLA_KV_LORA, MLA_HEADS, MLA_NOPE_DIM + MLA_V_DIM)
    w_knope = jnp.pad(w_kv[:, :, :MLA_NOPE_DIM], ((0, LANES), (0, 0), (0, LANES)))
    rope_place = jnp.eye(LANES, dk, k=LANES, dtype=F32) * (
        jnp.arange(LANES) < MLA_ROPE_DIM).astype(F32)[:, None]
    rope_rows = jnp.concatenate([jnp.zeros((MLA_KV_LORA, dk), F32), rope_place], axis=0)
    w_k = w_knope + rope_rows[:, None, :]
    w_v = jnp.pad(w_kv[:, :, MLA_NOPE_DIM:], ((0, LANES), (0, 0), (0, 0)))
    w_kv_all = jnp.concatenate([w_k.reshape(MLA_KV_LORA + LANES, MLA_HEADS * dk),
                                w_v.reshape(MLA_KV_LORA + LANES, MLA_HEADS * MLA_V_DIM)], axis=1)
    kv = _matmul(kin, w_kv_all.astype(BF16), seq=seq, tn=tn, name="mla_kv_proj")

    o = _attention(q, kv, kv, batch=batch, seq=seq, heads=MLA_HEADS, dk=dk, dv=MLA_V_DIM,
                   q_col=0, k_col=0, v_col=MLA_HEADS * dk // MLA_V_DIM, mode="plain",
                   name="mla_attn")
    return _matmul(o, w_out.astype(BF16), seq=seq, out_dtype=F32, epilogue="residual",
                   mod=mod, x_res=x, gate_row=2, name="l1_out_proj")


def kernel(x, c, ada_w, ada_b, norm_mix, norm_ffn, ab_w_in, ab_forget_bias, ab_w_out, mla_w_dq,
           mla_q_norm, mla_w_uq, mla_w_dkv, mla_kv_norm, mla_w_ukv, mla_w_out, router_w,
           router_bias, exp_w_gate, exp_w_up, exp_w_down, final_norm):
    batch, seq, d = x.shape
    depth = ada_w.shape[0]
    mod = _adaln(c, ada_w, ada_b)
    xs = x.reshape(batch * seq, d)
    for layer in range(depth):
        if layer % 2 == 0:
            xs = _moba_fox_mixer(xs, norm_mix[layer], mod[layer], ab_w_in, ab_forget_bias,
                                 ab_w_out, batch, seq)
        else:
            xs = _mla_mixer(xs, norm_mix[layer], mod[layer], mla_w_dq, mla_q_norm, mla_w_uq,
                            mla_w_dkv, mla_kv_norm, mla_w_ukv, mla_w_out, batch, seq)
        final_gain = final_norm if layer == depth - 1 else None
        xs = _moe_layer(xs, norm_ffn[layer], mod[layer], router_w, router_bias,
                        exp_w_gate, exp_w_up, exp_w_down, layer, seq, final_gain)
    return xs.reshape(batch, seq, d)
```

```python
import functools
import math

import jax
import jax.numpy as jnp
from jax import lax
from jax.experimental import pallas as pl
from jax.experimental.pallas import tpu as pltpu

F32 = jnp.float32
BF16 = jnp.bfloat16

HEAD_DIM = 128
MOBA_HEADS = 8
FOX_HEADS = 8
ROT_DIM = HEAD_DIM // 4
ROPE_THETA = 500000.0
MOBA_BLOCK = 256
MOBA_TOPK = 3
MLA_HEADS = 16
MLA_Q_LORA = 512
MLA_KV_LORA = 512
MLA_NOPE_DIM = 128
MLA_ROPE_DIM = 64
MLA_V_DIM = 128
N_EXPERTS = 16
N_GROUPS = 4
EXPERTS_PER_GROUP = N_EXPERTS // N_GROUPS
TOP_K = 2
NORM_EPS = 1e-6

LANES = 128
V7X_VMEM_BYTES = 64 * 1024 * 1024
MIB = 1024 * 1024
LOG2E = math.log2(math.e)

NT_DIMS = (((1,), (1,)), ((), ()))


def _cparams(semantics, vmem_mib):
    assert vmem_mib * MIB < V7X_VMEM_BYTES
    return pltpu.CompilerParams(dimension_semantics=semantics,
                                vmem_limit_bytes=vmem_mib * MIB)


def _pick(n, pref):
    t = min(pref, n)
    while n % t:
        t //= 2
    return t


def _adaln_kernel(c_ref, w_ref, b_ref, o_ref):
    c = c_ref[...]
    ca = (c * jax.nn.sigmoid(c)).astype(BF16)
    acc = jnp.dot(ca, w_ref[0].astype(BF16), preferred_element_type=F32)
    o_ref[0] = acc + b_ref[0]


def _adaln(c, ada_w, ada_b):
    depth, d, n6 = ada_w.shape
    b = c.shape[0]
    rows = 8
    c_pad = jnp.pad(c, ((0, rows - b), (0, 0)))
    tn = _pick(n6, 1024)
    out = pl.pallas_call(
        _adaln_kernel,
        out_shape=jax.ShapeDtypeStruct((depth, rows, n6), F32),
        grid=(depth, n6 // tn),
        in_specs=[
            pl.BlockSpec((rows, d), lambda l, j: (0, 0)),
            pl.BlockSpec((1, d, tn), lambda l, j: (l, 0, j)),
            pl.BlockSpec((1, 1, tn), lambda l, j: (l, 0, j)),
        ],
        out_specs=pl.BlockSpec((1, rows, tn), lambda l, j: (l, 0, j)),
        compiler_params=_cparams(("parallel", "parallel"), 40),
        name="adaln_mod",
    )(c_pad, ada_w, ada_b.reshape(depth, 1, n6))
    return out[:, :b, :].reshape(depth, b, 6, d)


NORM_ROWS = 256


def _norm_mod_rows(x_ref, gain_ref, mod_ref, shift_row, scale_row, store):
    tm = x_ref.shape[0]
    rows = min(NORM_ROWS, tm)
    gain = gain_ref[...]
    scale1p = 1.0 + mod_ref[0, scale_row:scale_row + 1, :]
    shift = mod_ref[0, shift_row:shift_row + 1, :]

    def body(c, carry):
        r0 = pl.multiple_of(c * rows, rows)
        x = x_ref[pl.ds(r0, rows), :]
        ms = jnp.mean(x * x, axis=-1, keepdims=True)
        y = x * lax.rsqrt(ms + NORM_EPS)
        h = (y * gain) * scale1p + shift
        store(r0, rows, h)
        return carry

    lax.fori_loop(0, tm // rows, body, 0)


def _rope_group(a, cos_t, sin_up, sin_dn, half):
    up = pltpu.roll(a, LANES - half, 1)
    dn = pltpu.roll(a, half, 1)
    return a * cos_t + up * sin_up + dn * sin_dn


def _rope_tables(seq, rot_dim):
    half = rot_dim // 2
    inv_freq = ROPE_THETA ** (-jnp.arange(half, dtype=F32) / half)
    ang = jnp.arange(seq, dtype=F32)[:, None] * inv_freq[None, :]
    cos, sin = jnp.cos(ang), jnp.sin(ang)
    zeros = jnp.zeros((seq, LANES - rot_dim), F32)
    zh = jnp.zeros((seq, half), F32)
    cos_t = jnp.concatenate([cos, cos, jnp.ones((seq, LANES - rot_dim), F32)], axis=1)
    sin_up = jnp.concatenate([-sin, zh, zeros], axis=1)
    sin_dn = jnp.concatenate([zh, sin, zeros], axis=1)
    return cos_t, sin_up, sin_dn


def _mm_kernel(*refs, prologue, epilogue, rot_groups, rot_tile_limit, rot_half,
               shift_row, scale_row, gate_row, has_side):
    it = iter(refs)
    a_ref = next(it)
    if prologue == "norm":
        gain_ref = next(it)
    if prologue == "norm" or epilogue == "residual":
        mod_ref = next(it)
    w_ref = next(it)
    if has_side:
        wside_ref = next(it)
    if epilogue == "rot":
        cos_ref, up_ref, dn_ref = next(it), next(it), next(it)
    if epilogue == "residual":
        x_ref = next(it)
    o_ref = next(it)
    if has_side:
        side_ref = next(it)
    if prologue == "norm":
        hb_ref = next(it)

    j = pl.program_id(1)

    if prologue == "norm":
        @pl.when(j == 0)
        def _():
            def store(r0, rows, h):
                hb_ref[pl.ds(r0, rows), :] = h.astype(BF16)
            _norm_mod_rows(a_ref, gain_ref, mod_ref, shift_row, scale_row, store)
            if has_side:
                side_ref[...] = jnp.dot(hb_ref[...], wside_ref[...],
                                        preferred_element_type=F32)
        lhs_ref = hb_ref
    else:
        lhs_ref = a_ref

    def product():
        return jnp.dot(lhs_ref[...], w_ref[...], preferred_element_type=F32)

    if epilogue == "rot":
        def rotated():
            acc = product()
            groups = []
            for g in range(acc.shape[1] // LANES):
                blk = acc[:, g * LANES:(g + 1) * LANES]
                if g in rot_groups:
                    blk = _rope_group(blk, cos_ref[...], up_ref[...], dn_ref[...], rot_half)
                groups.append(blk)
            return jnp.concatenate(groups, axis=1)

        if rot_tile_limit is None:
            o_ref[...] = rotated().astype(o_ref.dtype)
        else:
            @pl.when(j < rot_tile_limit)
            def _():
                o_ref[...] = rotated().astype(o_ref.dtype)

            @pl.when(j >= rot_tile_limit)
            def _():
                o_ref[...] = product().astype(o_ref.dtype)
    elif epilogue == "residual":
        gate = mod_ref[0, gate_row:gate_row + 1, :]
        o_ref[...] = x_ref[...] + gate * product()
    else:
        o_ref[...] = product().astype(o_ref.dtype)


def _matmul(a, w, *, seq, out_dtype=BF16, tm=1024, tn=512, prologue="none", gain=None,
            mod=None, shift_row=0, scale_row=1, epilogue="none", rot_tables=None,
            rot_groups=(), rot_tile_limit=None, rot_half=0, x_res=None, gate_row=2,
            w_side=None, name="matmul"):
    m, k = a.shape
    n = w.shape[1]
    tm = _pick(seq, tm)
    tn = _pick(n, tn)
    tiles_per_seq = seq // tm
    has_side = w_side is not None
    in_specs = [pl.BlockSpec((tm, k), lambda i, j: (i, 0))]
    args = [a]
    if prologue == "norm":
        in_specs.append(pl.BlockSpec((1, k), lambda i, j: (0, 0)))
        args.append(gain.reshape(1, k))
    if prologue == "norm":
        in_specs.append(pl.BlockSpec((1, 6, k), lambda i, j: (i // tiles_per_seq, 0, 0)))
        args.append(mod)
    elif epilogue == "residual":
        in_specs.append(pl.BlockSpec((1, 6, tn), lambda i, j: (i // tiles_per_seq, 0, j)))
        args.append(mod)
    in_specs.append(pl.BlockSpec((k, tn), lambda i, j: (0, j)))
    args.append(w)
    if has_side:
        in_specs.append(pl.BlockSpec((k, LANES), lambda i, j: (0, 0)))
        args.append(w_side)
    if epilogue == "rot":
        for t in rot_tables:
            in_specs.append(pl.BlockSpec((tm, LANES), lambda i, j: (i % tiles_per_seq, 0)))
            args.append(t)
    if epilogue == "residual":
        in_specs.append(pl.BlockSpec((tm, tn), lambda i, j: (i, j)))
        args.append(x_res)
    out_shape = [jax.ShapeDtypeStruct((m, n), out_dtype)]
    out_specs = [pl.BlockSpec((tm, tn), lambda i, j: (i, j))]
    if has_side:
        out_shape.append(jax.ShapeDtypeStruct((m, LANES), F32))
        out_specs.append(pl.BlockSpec((tm, LANES), lambda i, j: (i, 0)))
    scratch = [pltpu.VMEM((tm, k), BF16)] if prologue == "norm" else []
    kernel = functools.partial(
        _mm_kernel, prologue=prologue, epilogue=epilogue, rot_groups=tuple(rot_groups),
        rot_tile_limit=rot_tile_limit, rot_half=rot_half, shift_row=shift_row,
        scale_row=scale_row, gate_row=gate_row, has_side=has_side)
    outs = pl.pallas_call(
        kernel,
        out_shape=out_shape,
        grid=(m // tm, n // tn),
        in_specs=in_specs,
        out_specs=out_specs,
        scratch_shapes=scratch,
        compiler_params=_cparams(("parallel", "arbitrary"), 48),
        name=name,
    )(*args)
    return outs if has_side else outs[0]


def _fox_prep_kernel(flog_ref, bias_ref, qx_ref, kx_ref, carry_ref):
    tc = flog_ref.shape[0]

    @pl.when(pl.program_id(1) == 0)
    def _():
        carry_ref[...] = jnp.zeros_like(carry_ref)

    z = flog_ref[...] + bias_ref[...]
    logf = jnp.minimum(z, 0.0) - jnp.log1p(jnp.exp(-jnp.abs(z)))
    row = lax.broadcasted_iota(jnp.int32, (tc, tc), 0)
    col = lax.broadcasted_iota(jnp.int32, (tc, tc), 1)
    tri = jnp.where(col <= row, 1.0, 0.0).astype(F32)
    csum = jnp.dot(tri, logf, preferred_element_type=F32,
                   precision=lax.Precision.HIGHEST) + carry_ref[...]
    carry_ref[...] = csum[tc - 1:tc, :]
    csum = csum * LOG2E
    lane = lax.broadcasted_iota(jnp.int32, (tc, LANES), 1)
    one = jnp.ones((tc, LANES), F32)
    zero = jnp.zeros((tc, LANES), F32)
    for h in range(FOX_HEADS):
        colv = jnp.broadcast_to(csum[:, h:h + 1], (tc, LANES))
        hi = colv.astype(BF16).astype(F32)
        r1 = colv - hi
        mid = r1.astype(BF16).astype(F32)
        lo = r1 - mid
        qv = jnp.where(lane == 0, hi, jnp.where(lane == 1, mid, jnp.where(
            lane == 2, lo, jnp.where(lane < 6, one, zero))))
        kv = jnp.where(lane < 3, one, jnp.where(lane == 3, -hi, jnp.where(
            lane == 4, -mid, jnp.where(lane == 5, -lo, zero))))
        qx_ref[:, h * LANES:(h + 1) * LANES] = qv.astype(BF16)
        kx_ref[:, h * LANES:(h + 1) * LANES] = kv.astype(BF16)


def _fox_prep(flog, forget_bias, batch, seq):
    n = flog.shape[0]
    tc = _pick(seq, 256)
    bias = jnp.pad(forget_bias, (0, LANES - FOX_HEADS)).reshape(1, LANES)
    width = FOX_HEADS * LANES
    steps = seq // tc
    return pl.pallas_call(
        _fox_prep_kernel,
        out_shape=[jax.ShapeDtypeStruct((n, width), BF16)] * 2,
        grid=(batch, steps),
        in_specs=[pl.BlockSpec((tc, LANES), lambda b, s: (b * steps + s, 0)),
                  pl.BlockSpec((1, LANES), lambda b, s: (0, 0))],
        out_specs=[pl.BlockSpec((tc, width), lambda b, s: (b * steps + s, 0))] * 2,
        scratch_shapes=[pltpu.VMEM((1, LANES), F32)],
        compiler_params=_cparams(("parallel", "arbitrary"), 32),
        name="fox_prep",
    )(flog, bias)


ATTN_TILE = 512
ATTN_HEADS_PER_STEP = 8


M_INIT = -1e30


def _attn_kernel(qt_ref, kt_ref, *refs, mode, tq, dk, dv):
    it = iter(refs)
    q_ref, k_ref, v_ref = next(it), next(it), next(it)
    if mode == "fox":
        qx_ref, kx_ref = next(it), next(it)
    if mode == "moba":
        kmean_ref = next(it)
    o_ref = next(it)
    m_ref, l_ref, acc_ref = next(it), next(it), next(it)
    if mode == "moba":
        sel_ref = next(it)

    p_id = pl.program_id(2)
    i, j = qt_ref[p_id], kt_ref[p_id]
    neg_inf = jnp.float32(-jnp.inf)
    heads = q_ref.shape[1] // dk
    blocks_per_tile = tq // MOBA_BLOCK
    block_shift = MOBA_BLOCK.bit_length() - 1

    def q_of(h):
        q = q_ref[:, h * dk:(h + 1) * dk]
        if mode == "fox":
            q = jnp.concatenate([q, qx_ref[:, h * LANES:(h + 1) * LANES]], axis=1)
        return q

    def scores(h):
        k = k_ref[:, h * dk:(h + 1) * dk]
        if mode == "fox":
            k = jnp.concatenate([k, kx_ref[:, h * LANES:(h + 1) * LANES]], axis=1)
        return lax.dot_general(q_of(h), k, NT_DIMS, preferred_element_type=F32)

    def block_hits(h):
        lane = lax.broadcasted_iota(jnp.int32, (tq, LANES), 1)
        sel = sel_ref[h]
        return [jnp.broadcast_to(
            jnp.max(jnp.where(lane == j * blocks_per_tile + c, sel, 0.0), axis=1, keepdims=True),
            (tq, LANES)) > 0.0 for c in range(blocks_per_tile)]

    def masked(s, allowed_of_group):
        groups = [jnp.where(allowed_of_group(g), s[:, g * LANES:(g + 1) * LANES], neg_inf)
                  for g in range(tq // LANES)]
        return jnp.concatenate(groups, axis=1)

    def update(h, s):
        m_old = m_ref[h]
        row_max = jnp.broadcast_to(jnp.max(s, axis=1, keepdims=True), m_old.shape)
        m_new = jnp.maximum(m_old, row_max)
        alpha = jnp.exp2(m_old - m_new)
        p = jnp.exp2(s - jnp.concatenate([m_new] * (s.shape[1] // LANES), axis=1))
        row_sum = jnp.broadcast_to(jnp.sum(p, axis=1, keepdims=True), m_old.shape)
        l_ref[h] = alpha * l_ref[h] + row_sum
        acc_ref[h] = alpha * acc_ref[h] + jnp.dot(
            p.astype(BF16), v_ref[:, h * dv:(h + 1) * dv], preferred_element_type=F32)
        m_ref[h] = m_new

    @pl.when(j == 0)
    def _():
        m_ref[...] = jnp.full_like(m_ref, M_INIT)
        l_ref[...] = jnp.zeros_like(l_ref)
        acc_ref[...] = jnp.zeros_like(acc_ref)
        if mode == "moba":
            nb = kmean_ref.shape[0]
            blk = lax.broadcasted_iota(jnp.int32, (nb, tq), 0)
            blk_f = blk.astype(F32)
            query_block = jnp.right_shift(lax.broadcasted_iota(jnp.int32, (nb, tq), 1),
                                          block_shift)
            past = blk < i * blocks_per_tile + query_block
            for h in range(heads):
                gate = lax.dot_general(kmean_ref[:, h * dk:(h + 1) * dk], q_of(h).astype(F32),
                                       NT_DIMS, preferred_element_type=F32,
                                       precision=lax.Precision.HIGHEST)
                g = jnp.where(past, gate, neg_inf)
                chosen = jnp.zeros(gate.shape, F32)
                for _ in range(min(MOBA_TOPK, nb - 1)):
                    mx = jnp.max(g, axis=0, keepdims=True)
                    first = jnp.min(jnp.where(g == mx, blk_f, float(LANES)), axis=0,
                                    keepdims=True)
                    pick = blk_f == first
                    chosen = jnp.where(jnp.logical_and(pick, past), 1.0, chosen)
                    g = jnp.where(pick, neg_inf, g)
                chosen = jnp.concatenate([chosen, jnp.zeros((LANES - nb, tq), F32)], axis=0)
                sel_ref[h] = chosen.T

    groups_per_block = MOBA_BLOCK // LANES

    @pl.when(j < i)
    def _():
        for h in range(heads):
            s = scores(h)
            if mode == "moba":
                hits = block_hits(h)
                s = masked(s, lambda g: hits[g // groups_per_block])
            update(h, s)

    @pl.when(j == i)
    def _():
        row = lax.broadcasted_iota(jnp.int32, (tq, LANES), 0)
        lane = lax.broadcasted_iota(jnp.int32, (tq, LANES), 1)
        row_blk = jnp.right_shift(row, block_shift)
        for h in range(heads):
            s = scores(h)
            if mode == "moba":
                hits = block_hits(h)

                def allowed(g):
                    col_blk = g // groups_per_block
                    own = jnp.logical_and(row_blk == col_blk, lane + g * LANES <= row)
                    return jnp.logical_or(own, jnp.logical_and(row_blk > col_blk, hits[col_blk]))
            else:
                def allowed(g):
                    return lane + g * LANES <= row
            update(h, masked(s, allowed))
            o_ref[:, h * dv:(h + 1) * dv] = (acc_ref[h] / l_ref[h]).astype(o_ref.dtype)


def _attention(q_arr, k_arr, v_arr, *, batch, seq, heads, dk, dv, q_col, k_col, v_col, mode,
               qx=None, kx=None, kmean=None, name="attn"):
    n = batch * seq
    tq = _pick(seq, ATTN_TILE)
    g = ATTN_HEADS_PER_STEP
    assert tq % MOBA_BLOCK == 0 and heads % g == 0
    assert q_col % g == 0 and k_col % g == 0 and v_col % g == 0
    nq = seq // tq
    pairs = [(i, j) for i in range(nq) for j in range(i + 1)]
    q_tile = jnp.array([p[0] for p in pairs], jnp.int32)
    k_tile = jnp.array([p[1] for p in pairs], jnp.int32)

    def q_rows(b, h, p, qt, kt):
        return b * nq + qt[p]

    def k_rows(b, h, p, qt, kt):
        return b * nq + kt[p]

    in_specs = [
        pl.BlockSpec((tq, g * dk), lambda b, h, p, qt, kt: (q_rows(b, h, p, qt, kt), q_col // g + h)),
        pl.BlockSpec((tq, g * dk), lambda b, h, p, qt, kt: (k_rows(b, h, p, qt, kt), k_col // g + h)),
        pl.BlockSpec((tq, g * dv), lambda b, h, p, qt, kt: (k_rows(b, h, p, qt, kt), v_col // g + h)),
    ]
    args = [q_arr, k_arr, v_arr]
    if mode == "fox":
        in_specs += [
            pl.BlockSpec((tq, g * LANES), lambda b, h, p, qt, kt: (q_rows(b, h, p, qt, kt), h)),
            pl.BlockSpec((tq, g * LANES), lambda b, h, p, qt, kt: (k_rows(b, h, p, qt, kt), h))]
        args += [qx, kx]
    if mode == "moba":
        nb = seq // MOBA_BLOCK
        in_specs.append(pl.BlockSpec((nb, g * dk), lambda b, h, p, qt, kt: (b, h)))
        args.append(kmean)
    assert dv == LANES
    scratch = [pltpu.VMEM((g, tq, LANES), F32), pltpu.VMEM((g, tq, LANES), F32),
               pltpu.VMEM((g, tq, dv), F32)]
    if mode == "moba":
        scratch.append(pltpu.VMEM((g, tq, LANES), F32))
    return pl.pallas_call(
        functools.partial(_attn_kernel, mode=mode, tq=tq, dk=dk, dv=dv),
        out_shape=jax.ShapeDtypeStruct((n, heads * dv), BF16),
        grid_spec=pltpu.PrefetchScalarGridSpec(
            num_scalar_prefetch=2,
            grid=(batch, heads // g, len(pairs)),
            in_specs=in_specs,
            out_specs=pl.BlockSpec((tq, g * dv),
                                   lambda b, h, p, qt, kt: (q_rows(b, h, p, qt, kt), h)),
            scratch_shapes=scratch,
        ),
        compiler_params=_cparams(("parallel", "parallel", "arbitrary"), 40),
        name=name,
    )(q_tile, k_tile, *args)


def _block_means_kernel(k_ref, o_ref):
    for b in range(o_ref.shape[0]):
        kb = k_ref[b * MOBA_BLOCK:(b + 1) * MOBA_BLOCK, :].astype(F32)
        o_ref[b:b + 1, :] = jnp.mean(kb, axis=0, keepdims=True)


def _block_means(arr, col0, cols):
    n = arr.shape[0]
    blocks = 8
    rows = blocks * MOBA_BLOCK
    assert n % rows == 0 and col0 % cols == 0
    return pl.pallas_call(
        _block_means_kernel,
        out_shape=jax.ShapeDtypeStruct((n // MOBA_BLOCK, cols), F32),
        grid=(n // rows,),
        in_specs=[pl.BlockSpec((rows, cols), lambda i: (i, col0 // cols))],
        out_specs=pl.BlockSpec((blocks, cols), lambda i: (i, 0)),
        compiler_params=_cparams(("parallel",), 32),
        name="moba_block_means",
    )(arr)


def _top2_of4(vals):
    a, b, c, d = vals
    hi1, lo1 = jnp.maximum(a, b), jnp.minimum(a, b)
    hi2, lo2 = jnp.maximum(c, d), jnp.minimum(c, d)
    return jnp.maximum(hi1, hi2) + jnp.maximum(jnp.minimum(hi1, hi2), jnp.maximum(lo1, lo2))


def _route_kernel(x_ref, gain_ref, mod_ref, rwt_ref, rbias_ref, h_ref, idx_ref, wgt_ref):
    def store(r0, rows, h):
        h_ref[pl.ds(r0, rows), :] = h
    _norm_mod_rows(x_ref, gain_ref, mod_ref, 3, 4, store)

    logits = lax.dot_general(rwt_ref[...], h_ref[...], NT_DIMS, preferred_element_type=F32,
                             precision=lax.Precision.HIGHEST)
    score = jax.nn.sigmoid(logits)
    biased = score + rbias_ref[...]
    rows_b = [biased[e:e + 1, :] for e in range(N_EXPERTS)]
    rows_s = [score[e:e + 1, :] for e in range(N_EXPERTS)]
    grp = [_top2_of4(rows_b[g * EXPERTS_PER_GROUP:(g + 1) * EXPERTS_PER_GROUP])
           for g in range(N_GROUPS)]
    best, g_sel = grp[0], jnp.zeros_like(grp[0], dtype=jnp.int32)
    for g in range(1, N_GROUPS):
        better = grp[g] > best
        best = jnp.where(better, grp[g], best)
        g_sel = jnp.where(better, g, g_sel)

    def in_group(rows, r):
        out = rows[r]
        for g in range(1, N_GROUPS):
            out = jnp.where(g_sel == g, rows[g * EXPERTS_PER_GROUP + r], out)
        return out

    cb = [in_group(rows_b, r) for r in range(EXPERTS_PER_GROUP)]
    cs = [in_group(rows_s, r) for r in range(EXPERTS_PER_GROUP)]
    neg_inf = jnp.float32(-jnp.inf)
    picks = []
    for _ in range(TOP_K):
        bv, bi, bs = cb[0], jnp.zeros_like(g_sel), cs[0]
        for r in range(1, EXPERTS_PER_GROUP):
            better = cb[r] > bv
            bv = jnp.where(better, cb[r], bv)
            bi = jnp.where(better, r, bi)
            bs = jnp.where(better, cs[r], bs)
        picks.append((bi, bs))
        cb = [jnp.where(bi == r, neg_inf, cb[r]) for r in range(EXPERTS_PER_GROUP)]
    (i0, s0), (i1, s1) = picks
    total = s0 + s1
    idx_ref[0:1, :] = g_sel * EXPERTS_PER_GROUP + i0
    idx_ref[1:2, :] = g_sel * EXPERTS_PER_GROUP + i1
    wgt_ref[0:1, :] = s0 / total
    wgt_ref[1:2, :] = s1 / total


def _route(x, gain, mod, router_w, router_bias, seq):
    n, d = x.shape
    tm = _pick(seq, 512)
    tiles_per_seq = seq // tm
    return pl.pallas_call(
        _route_kernel,
        out_shape=[jax.ShapeDtypeStruct((n, d), F32),
                   jax.ShapeDtypeStruct((TOP_K, n), jnp.int32),
                   jax.ShapeDtypeStruct((TOP_K, n), F32)],
        grid=(n // tm,),
        in_specs=[pl.BlockSpec((tm, d), lambda i: (i, 0)),
                  pl.BlockSpec((1, d), lambda i: (0, 0)),
                  pl.BlockSpec((1, 6, d), lambda i: (i // tiles_per_seq, 0, 0)),
                  pl.BlockSpec((N_EXPERTS, d), lambda i: (0, 0)),
                  pl.BlockSpec((N_EXPERTS, 1), lambda i: (0, 0))],
        out_specs=[pl.BlockSpec((tm, d), lambda i: (i, 0)),
                   pl.BlockSpec((TOP_K, tm), lambda i: (0, i)),
                   pl.BlockSpec((TOP_K, tm), lambda i: (0, i))],
        compiler_params=_cparams(("parallel",), 32),
        name="moe_route",
    )(x, gain.reshape(1, d), mod, router_w.T, router_bias.reshape(N_EXPERTS, 1))


GMM_ROWS = 512
GMM_UP_COLS = 128
GMM_ISSUE_STEPS = 8
PACK_ROWS = 256
DMA_UNROLL = 8


def _pack_kernel(wg_ref, wu_ref, wd_ref, wgu_ref, wdb_ref):
    n_up = wgu_ref.shape[1]
    for f in range(n_up):
        cols = slice(f * GMM_UP_COLS, (f + 1) * GMM_UP_COLS)
        wgu_ref[0, f, :, :GMM_UP_COLS] = wg_ref[0, 0, :, cols].astype(BF16)
        wgu_ref[0, f, :, GMM_UP_COLS:] = wu_ref[0, 0, :, cols].astype(BF16)
    wdb_ref[0] = wd_ref[0, 0].astype(BF16)


def _pack_expert_weights(w_gate, w_up, w_down, layer):
    _, n_exp, d, d_exp = w_gate.shape
    n_up = d_exp // GMM_UP_COLS
    steps = d // PACK_ROWS
    down_rows = d_exp // steps
    assert d_exp % GMM_UP_COLS == 0 and d % PACK_ROWS == 0 and down_rows % 16 == 0
    up_spec = pl.BlockSpec((1, 1, PACK_ROWS, d_exp), lambda e, k: (layer, e, k, 0))
    return pl.pallas_call(
        _pack_kernel,
        out_shape=[jax.ShapeDtypeStruct((n_exp, n_up, d, 2 * GMM_UP_COLS), BF16),
                   jax.ShapeDtypeStruct((n_exp, d_exp, d), BF16)],
        grid=(n_exp, steps),
        in_specs=[up_spec, up_spec,
                  pl.BlockSpec((1, 1, down_rows, d), lambda e, k: (layer, e, k, 0))],
        out_specs=[pl.BlockSpec((1, n_up, PACK_ROWS, 2 * GMM_UP_COLS), lambda e, k: (e, 0, k, 0)),
                   pl.BlockSpec((1, down_rows, d), lambda e, k: (e, k, 0))],
        compiler_params=_cparams(("parallel", "parallel"), 32),
        name="moe_pack_weights",
    )(w_gate, w_up, w_down)


def _ffn_kernel(eid_ref, nvalid_ref, dst_ref, h_hbm, wgu_ref, wd_ref, y_hbm,
                xbuf_ref, hb_ref, act_ref, ystage_ref, gsem, ssem):
    t = pl.program_id(0)
    rows = hb_ref.shape[0]
    n_up = act_ref.shape[0]
    n_tokens = h_hbm.shape[0]
    per_chunk = rows // GMM_ISSUE_STEPS
    nv = nvalid_ref[0]
    valid = t < nv

    def dst_of(tile, r):
        return dst_ref[(tile + 1) * rows + r]

    def token_of(dst_row):
        if n_tokens & (n_tokens - 1) == 0:
            return jnp.bitwise_and(dst_row, n_tokens - 1)
        return lax.rem(dst_row, n_tokens)

    def gather_row(r, src_row):
        return pltpu.make_async_copy(h_hbm.at[pl.ds(src_row, 1), :],
                                     xbuf_ref.at[pl.ds(r, 1), :], gsem)

    def scatter_row(r, dst_row):
        return pltpu.make_async_copy(ystage_ref.at[pl.ds(r, 1), :],
                                     y_hbm.at[pl.ds(dst_row, 1), :], ssem)

    def for_rows(fn):
        def body(r, carry):
            fn(r)
            return carry
        lax.fori_loop(0, rows, body, 0, unroll=DMA_UNROLL)

    @pl.when(t == 0)
    def _():
        ystage_ref[...] = jnp.zeros_like(ystage_ref)
        for_rows(lambda r: gather_row(r, token_of(dst_of(0, r))).start())

    @pl.when(t <= nv)
    def _():
        for_rows(lambda r: gather_row(r, 0).wait())

    @pl.when(t == nv)
    def _():
        for_rows(lambda r: scatter_row(r, dst_of(t - 1, r)).start())
        for_rows(lambda r: scatter_row(r, 0).wait())

    @pl.when(valid)
    def _():
        hb_ref[...] = xbuf_ref[...].astype(BF16)
        for f in range(n_up):
            if f < GMM_ISSUE_STEPS:
                for r in range(f * per_chunk, (f + 1) * per_chunk):
                    gather_row(r, token_of(dst_of(t + 1, r))).start()
                    scatter_row(r, dst_of(t - 1, r)).start()
            gu = jnp.dot(hb_ref[...], wgu_ref[0, f], preferred_element_type=F32)
            g, u = gu[:, :GMM_UP_COLS], gu[:, GMM_UP_COLS:]
            act_ref[f] = ((g * jax.nn.sigmoid(g)) * u).astype(BF16)
        for_rows(lambda r: scatter_row(r, 0).wait())
        act = jnp.concatenate([act_ref[k] for k in range(n_up)], axis=1)
        ystage_ref[...] = jnp.dot(act, wd_ref[0], preferred_element_type=F32)


def _expert_ffn(eid, nvalid, dst, h, wgu, wdb, n_slots):
    n_tiles = dst.shape[0] // GMM_ROWS - 1
    d = h.shape[1]
    n_exp, n_up, _, up_cols = wgu.shape
    d_exp = wdb.shape[1]
    assert GMM_ROWS % GMM_ISSUE_STEPS == 0 and n_up >= GMM_ISSUE_STEPS
    return pl.pallas_call(
        _ffn_kernel,
        out_shape=jax.ShapeDtypeStruct((n_slots + GMM_ROWS, d), F32),
        grid_spec=pltpu.PrefetchScalarGridSpec(
            num_scalar_prefetch=3,
            grid=(n_tiles,),
            in_specs=[
                pl.BlockSpec(memory_space=pl.ANY),
                pl.BlockSpec((1, n_up, d, up_cols), lambda t, e, nv, ds: (e[t], 0, 0, 0)),
                pl.BlockSpec((1, d_exp, d), lambda t, e, nv, ds: (e[t], 0, 0)),
            ],
            out_specs=pl.BlockSpec(memory_space=pl.ANY),
            scratch_shapes=[pltpu.VMEM((GMM_ROWS, d), F32),
                            pltpu.VMEM((GMM_ROWS, d), BF16),
                            pltpu.VMEM((n_up, GMM_ROWS, GMM_UP_COLS), BF16),
                            pltpu.VMEM((GMM_ROWS, d), F32),
                            pltpu.SemaphoreType.DMA,
                            pltpu.SemaphoreType.DMA],
        ),
        compiler_params=_cparams(("arbitrary",), 56),
        name="moe_ffn",
    )(eid, nvalid, dst, h, wgu, wdb)


def _combine_kernel(x_ref, y0_ref, y1_ref, w_ref, mod_ref, *rest, final):
    if final:
        gain_ref, o_ref = rest
    else:
        (o_ref,) = rest
    rows = min(NORM_ROWS, x_ref.shape[0])
    gate = mod_ref[0, 5:6, :]

    def body(c, carry):
        r0 = pl.multiple_of(c * rows, rows)
        w = w_ref[pl.ds(r0, rows), :]
        y = (w[:, 0:1] * y0_ref[pl.ds(r0, rows), :]
             + w[:, 1:2] * y1_ref[pl.ds(r0, rows), :])
        out = x_ref[pl.ds(r0, rows), :] + gate * y
        if final:
            ms = jnp.mean(out * out, axis=-1, keepdims=True)
            out = (out * lax.rsqrt(ms + NORM_EPS)) * gain_ref[...]
        o_ref[pl.ds(r0, rows), :] = out
        return carry

    lax.fori_loop(0, x_ref.shape[0] // rows, body, 0)


def _combine(x, y_slots, wgt, mod, seq, final_gain=None):
    n, d = x.shape
    tm = _pick(seq, 512)
    tiles_per_seq = seq // tm
    tiles = n // tm
    final = final_gain is not None
    in_specs = [pl.BlockSpec((tm, d), lambda i: (i, 0)),
                pl.BlockSpec((tm, d), lambda i: (i, 0)),
                pl.BlockSpec((tm, d), lambda i: (tiles + i, 0)),
                pl.BlockSpec((tm, TOP_K), lambda i: (i, 0)),
                pl.BlockSpec((1, 6, d), lambda i: (i // tiles_per_seq, 0, 0))]
    args = [x, y_slots, y_slots, wgt, mod]
    if final:
        in_specs.append(pl.BlockSpec((1, d), lambda i: (0, 0)))
        args.append(final_gain.reshape(1, d))
    return pl.pallas_call(
        functools.partial(_combine_kernel, final=final),
        out_shape=jax.ShapeDtypeStruct((n, d), F32),
        grid=(tiles,),
        in_specs=in_specs,
        out_specs=pl.BlockSpec((tm, d), lambda i: (i, 0)),
        compiler_params=_cparams(("parallel",), 48),
        name="moe_combine",
    )(*args)


def _dispatch_plan(idx):
    n = idx.shape[1]
    slots = TOP_K * n
    cap = slots + N_EXPERTS * GMM_ROWS
    expert = idx.reshape(slots)
    onehot = (expert[:, None] == jnp.arange(N_EXPERTS)[None, :]).astype(jnp.int32)
    csum = jnp.cumsum(onehot, axis=0)
    rank = jnp.sum(csum * onehot, axis=1) - 1
    counts = csum[-1]
    padded = -(-counts // GMM_ROWS) * GMM_ROWS
    ends = jnp.cumsum(padded)
    pos = (ends - padded)[expert] + rank
    slot_ids = jnp.arange(slots, dtype=jnp.int32)
    dummy = slots + jnp.arange(cap + GMM_ROWS, dtype=jnp.int32) % GMM_ROWS
    dst = dummy.at[pos + GMM_ROWS].set(slot_ids)
    nvalid = (ends[-1] // GMM_ROWS).astype(jnp.int32)
    tile_start = jnp.arange(cap // GMM_ROWS, dtype=jnp.int32) * GMM_ROWS
    tile_start = jnp.minimum(tile_start, ends[-1] - GMM_ROWS)
    eid = jnp.sum(tile_start[:, None] >= ends[None, :], axis=1).astype(jnp.int32)
    return dst, eid, nvalid.reshape(1)


def _moe_layer(x, gain, mod, router_w, router_bias, w_gate, w_up, w_down, layer, seq,
               final_gain):
    wgu, wdb = _pack_expert_weights(w_gate, w_up, w_down, layer)
    h, idx, wgt = _route(x, gain, mod, router_w, router_bias, seq)
    dst, eid, nvalid = _dispatch_plan(idx)
    y_slots = _expert_ffn(eid, nvalid, dst, h, wgu, wdb, TOP_K * x.shape[0])
    return _combine(x, y_slots, wgt.T, mod, seq, final_gain)


def _moba_fox_mixer(x, gain, mod, w_in, forget_bias, w_out, batch, seq):
    n, d = x.shape
    mix = (MOBA_HEADS + FOX_HEADS) * HEAD_DIM
    qk_scale = HEAD_DIM ** -0.5 * LOG2E
    s1 = MOBA_HEADS * HEAD_DIM
    s2 = FOX_HEADS * HEAD_DIM
    col_scale = jnp.ones((3 * mix,), F32)
    col_scale = col_scale.at[:s1].set(qk_scale).at[3 * s1:3 * s1 + s2].set(qk_scale)
    w_main = (w_in[:, :3 * mix] * col_scale[None, :]).astype(BF16)
    w_forget = jnp.pad(w_in[:, 3 * mix:], ((0, 0), (0, LANES - FOX_HEADS))).astype(BF16)
    tn = 1024
    proj, flog = _matmul(
        x, w_main, seq=seq, prologue="norm", gain=gain, mod=mod, shift_row=0, scale_row=1,
        epilogue="rot", rot_tables=_rope_tables(seq, ROT_DIM), rot_groups=range(tn // LANES),
        rot_tile_limit=2 * s1 // tn, rot_half=ROT_DIM // 2, w_side=w_forget, tn=tn,
        name="l0_in_proj")
    qx, kx = _fox_prep(flog, forget_bias, batch, seq)
    o_moba = _attention(proj, proj, proj, batch=batch, seq=seq, heads=MOBA_HEADS,
                        dk=HEAD_DIM, dv=HEAD_DIM, q_col=0, k_col=MOBA_HEADS,
                        v_col=2 * MOBA_HEADS, mode="moba", kmean=_block_means(proj, s1, s1),
                        name="moba_attn")
    base = 3 * MOBA_HEADS
    o_fox = _attention(proj, proj, proj, batch=batch, seq=seq, heads=FOX_HEADS,
                       dk=HEAD_DIM, dv=HEAD_DIM, q_col=base, k_col=base + FOX_HEADS,
                       v_col=base + 2 * FOX_HEADS, mode="fox", qx=qx, kx=kx,
                       name="fox_attn")
    o = jnp.concatenate([o_moba, o_fox], axis=1)
    return _matmul(o, w_out.astype(BF16), seq=seq, out_dtype=F32, epilogue="residual",
                   mod=mod, x_res=x, gate_row=2, tn=1024, name="l0_out_proj")


def _mla_down_kernel(x_ref, gain_ref, mod_ref, wd_ref, qn_ref, kvn_ref, cos_ref, up_ref, dn_ref,
                     cq_ref, kin_ref, hb_ref):
    def store(r0, rows, h):
        hb_ref[pl.ds(r0, rows), :] = h.astype(BF16)
    _norm_mod_rows(x_ref, gain_ref, mod_ref, 0, 1, store)
    a = jnp.dot(hb_ref[...], wd_ref[...], preferred_element_type=F32)

    def rms(v, g):
        ms = jnp.mean(v * v, axis=-1, keepdims=True)
        return (v * lax.rsqrt(ms + NORM_EPS)) * g

    q_end = MLA_Q_LORA
    kv_end = MLA_Q_LORA + MLA_KV_LORA
    cq_ref[...] = rms(a[:, :q_end], qn_ref[...]).astype(BF16)
    kin_ref[:, :MLA_KV_LORA] = rms(a[:, q_end:kv_end], kvn_ref[...]).astype(BF16)
    kpe = _rope_group(a[:, kv_end:kv_end + LANES], cos_ref[...], up_ref[...], dn_ref[...],
                      MLA_ROPE_DIM // 2)
    kin_ref[:, MLA_KV_LORA:] = kpe.astype(BF16)


def _mla_down(x, gain, mod, w_down, q_norm, kv_norm, tables, seq):
    n, d = x.shape
    tm = _pick(seq, 512)
    tiles_per_seq = seq // tm
    wcols = w_down.shape[1]
    const = lambda i: (0, 0)
    return pl.pallas_call(
        _mla_down_kernel,
        out_shape=[jax.ShapeDtypeStruct((n, MLA_Q_LORA), BF16),
                   jax.ShapeDtypeStruct((n, MLA_KV_LORA + LANES), BF16)],
        grid=(n // tm,),
        in_specs=[pl.BlockSpec((tm, d), lambda i: (i, 0)),
                  pl.BlockSpec((1, d), const),
                  pl.BlockSpec((1, 6, d), lambda i: (i // tiles_per_seq, 0, 0)),
                  pl.BlockSpec((d, wcols), const),
                  pl.BlockSpec((1, MLA_Q_LORA), const),
                  pl.BlockSpec((1, MLA_KV_LORA), const)]
                 + [pl.BlockSpec((tm, LANES), lambda i: (i % tiles_per_seq, 0))] * 3,
        out_specs=[pl.BlockSpec((tm, MLA_Q_LORA), lambda i: (i, 0)),
                   pl.BlockSpec((tm, MLA_KV_LORA + LANES), lambda i: (i, 0))],
        scratch_shapes=[pltpu.VMEM((tm, d), BF16)],
        compiler_params=_cparams(("parallel",), 40),
        name="mla_down",
    )(x, gain.reshape(1, d), mod, w_down, q_norm.reshape(1, -1), kv_norm.reshape(1, -1), *tables)


def _mla_mixer(x, gain, mod, w_dq, q_norm, w_uq, w_dkv, kv_norm, w_ukv, w_out, batch, seq):
    n, d = x.shape
    dk = 2 * LANES
    pad_rope = LANES - MLA_ROPE_DIM
    qk_scale = (MLA_NOPE_DIM + MLA_ROPE_DIM) ** -0.5 * LOG2E
    tables = _rope_tables(seq, MLA_ROPE_DIM)
    w_down = jnp.concatenate(
        [w_dq, w_dkv, jnp.zeros((d, pad_rope), F32)], axis=1).astype(BF16)
    cq, kin = _mla_down(x, gain, mod, w_down, q_norm, kv_norm, tables, seq)

    w_q = (w_uq * qk_scale).reshape(MLA_Q_LORA, MLA_HEADS, MLA_NOPE_DIM + MLA_ROPE_DIM)
    w_q = jnp.pad(w_q, ((0, 0), (0, 0), (0, pad_rope))).reshape(MLA_Q_LORA, MLA_HEADS * dk)
    tn = 1024
    q = _matmul(cq, w_q.astype(BF16), seq=seq, epilogue="rot", rot_tables=tables,
                rot_groups=range(1, tn // LANES, 2), rot_half=MLA_ROPE_DIM // 2, tn=tn,
                name="mla_q_proj")

    w_kv = w_ukv.reshape(MLA_KV_LORA, MLA_HEADS, MLA_NOPE_DIM + MLA_V_DIM)
    w_knope = jnp.pad(w_kv[:, :, :MLA_NOPE_DIM], ((0, LANES), (0, 0), (0, LANES)))
    rope_place = jnp.eye(LANES, dk, k=LANES, dtype=F32) * (
        jnp.arange(LANES) < MLA_ROPE_DIM).astype(F32)[:, None]
    rope_rows = jnp.concatenate([jnp.zeros((MLA_KV_LORA, dk), F32), rope_place], axis=0)
    w_k = w_knope + rope_rows[:, None, :]
    w_v = jnp.pad(w_kv[:, :, MLA_NOPE_DIM:], ((0, LANES), (0, 0), (0, 0)))
    w_kv_all = jnp.concatenate([w_k.reshape(MLA_KV_LORA + LANES, MLA_HEADS * dk),
                                w_v.reshape(MLA_KV_LORA + LANES, MLA_HEADS * MLA_V_DIM)], axis=1)
    kv = _matmul(kin, w_kv_all.astype(BF16), seq=seq, tn=tn, name="mla_kv_proj")

    o = _attention(q, kv, kv, batch=batch, seq=seq, heads=MLA_HEADS, dk=dk, dv=MLA_V_DIM,
                   q_col=0, k_col=0, v_col=MLA_HEADS * dk // MLA_V_DIM, mode="plain",
                   name="mla_attn")
    return _matmul(o, w_out.astype(BF16), seq=seq, out_dtype=F32, epilogue="residual",
                   mod=mod, x_res=x, gate_row=2, tn=1024, name="l1_out_proj")


def kernel(x, c, ada_w, ada_b, norm_mix, norm_ffn, ab_w_in, ab_forget_bias, ab_w_out, mla_w_dq,
           mla_q_norm, mla_w_uq, mla_w_dkv, mla_kv_norm, mla_w_ukv, mla_w_out, router_w,
           router_bias, exp_w_gate, exp_w_up, exp_w_down, final_norm):
    batch, seq, d = x.shape
    depth = ada_w.shape[0]
    mod = _adaln(c, ada_w, ada_b)
    xs = x.reshape(batch * seq, d)
    for layer in range(depth):
        if layer % 2 == 0:
            xs = _moba_fox_mixer(xs, norm_mix[layer], mod[layer], ab_w_in, ab_forget_bias,
                                 ab_w_out, batch, seq)
        else:
            xs = _mla_mixer(xs, norm_mix[layer], mod[layer], mla_w_dq, mla_q_norm, mla_w_uq,
                            mla_w_dkv, mla_kv_norm, mla_w_ukv, mla_w_out, batch, seq)
        final_gain = final_norm if layer == depth - 1 else None
        xs = _moe_layer(xs, norm_ffn[layer], mod[layer], router_w, router_bias,
                        exp_w_gate, exp_w_up, exp_w_down, layer, seq, final_gain)
    return xs.reshape(batch, seq, d)
```
